```python
import jax, jax.numpy as jnp
from jax import lax
import numpy as np

D_MODEL = 1024
BATCH = 2
SEQ = 8192
DEPTH = 1

LRU_WIDTH = D_MODEL
LRU_BLOCKS = 16
LRU_BLOCK_W = LRU_WIDTH // LRU_BLOCKS
LRU_C = 8.0
CONV_W = 4
ATT_HEADS = 16
ATT_HEAD_DIM = 64
ATT_WIDTH = ATT_HEADS * ATT_HEAD_DIM
Q_BLOCK = 128
D_FF = 4 * D_MODEL
PLE_DIM = 256
NORM_EPS = 1e-6

IN_SPLITS = [LRU_WIDTH, LRU_WIDTH, ATT_WIDTH, ATT_WIDTH, ATT_WIDTH, D_MODEL, D_MODEL]
IN_WIDTH = sum(IN_SPLITS)

kernel_name = 'hawk_stickbreak_hybrid'


def rmsnorm(x, g):
    x32 = x.astype(jnp.float32)
    y = x32 * lax.rsqrt(jnp.mean(x32 * x32, axis=-1, keepdims=True) + NORM_EPS)
    return (y * g.astype(jnp.float32)).astype(x.dtype)


def causal_depthwise_conv(x, w, b):
    S = x.shape[1]
    xp = jnp.pad(x, ((0, 0), (CONV_W - 1, 0), (0, 0)))
    y = b
    for k in range(CONV_W):
        y = y + w[k] * xp[:, k:k + S]
    return y


def rg_lru(x, w_r, b_r, w_i, b_i, lam):
    B, S, W = x.shape
    xb = x.reshape(B, S, LRU_BLOCKS, LRU_BLOCK_W)
    r = jax.nn.sigmoid((jnp.einsum('bsnc,ncd->bsnd', xb, w_r).reshape(B, S, W) + b_r).astype(jnp.float32))
    i = jax.nn.sigmoid((jnp.einsum('bsnc,ncd->bsnd', xb, w_i).reshape(B, S, W) + b_i).astype(jnp.float32))
    log_a = -LRU_C * r * jax.nn.softplus(-lam.astype(jnp.float32))
    a = jnp.exp(log_a)
    mult = jnp.sqrt(-jnp.expm1(2.0 * log_a))
    u = mult * (i * x.astype(jnp.float32))

    def combine(left, right):
        a_l, u_l = left
        a_r, u_r = right
        return a_l * a_r, a_r * u_l + u_r

    _, h = lax.associative_scan(combine, (a, u), axis=1)
    return h.astype(x.dtype)


def stick_breaking_attention(q, k, v):
    B, H, S, Dh = q.shape
    n_blk = S // Q_BLOCK
    scale = Dh ** -0.5
    q_blocks = q.reshape(B, H, n_blk, Q_BLOCK, Dh).transpose(2, 0, 1, 3, 4)
    key_pos = jnp.arange(S)

    def one_block(args):
        q_blk, blk = args
        z = jnp.einsum('bhqd,bhkd->bhqk', q_blk, k).astype(jnp.float32) * scale
        q_pos = blk * Q_BLOCK + jnp.arange(Q_BLOCK)
        causal = key_pos[None, :] < q_pos[:, None]
        log_1m = jnp.where(causal, jax.nn.log_sigmoid(-z), 0.0)
        suffix = lax.cumsum(log_1m, axis=3, reverse=True) - log_1m
        att = jnp.where(causal, jnp.exp(jax.nn.log_sigmoid(z) + suffix), 0.0)
        return jnp.einsum('bhqk,bhkd->bhqd', att.astype(v.dtype), v)

    out = lax.map(one_block, (q_blocks, jnp.arange(n_blk)))
    return out.transpose(1, 2, 0, 3, 4).reshape(B, H, S, Dh)


def setup_inputs(seed: int = 0) -> dict:
    key = jax.random.key(seed)
    ks = jax.random.split(key, 24)
    f32 = jnp.float32

    def nrm(k, shape, fan_in):
        return jax.random.normal(k, shape, f32) * (fan_in ** -0.5)

    def gain(k, shape):
        return 1.0 + 0.02 * jax.random.normal(k, shape, f32)

    u = jax.random.uniform(ks[10], (DEPTH, LRU_WIDTH), f32, 0.9, 0.999)
    a0 = u ** (1.0 / LRU_C)
    lam = jnp.log(a0) - jnp.log1p(-a0)
    return {
        'x': jax.random.normal(ks[0], (BATCH, SEQ, D_MODEL), f32),
        'p': jax.random.normal(ks[1], (DEPTH, BATCH, SEQ, PLE_DIM), f32),
        'norm_mix_g': gain(ks[2], (DEPTH, D_MODEL)),
        'w_in': nrm(ks[3], (DEPTH, D_MODEL, IN_WIDTH), D_MODEL),
        'conv_w': nrm(ks[4], (DEPTH, CONV_W, LRU_WIDTH), CONV_W),
        'conv_b': 0.02 * jax.random.normal(ks[5], (DEPTH, LRU_WIDTH), f32),
        'w_rgate': nrm(ks[6], (DEPTH, LRU_BLOCKS, LRU_BLOCK_W, LRU_BLOCK_W), LRU_BLOCK_W),
        'b_rgate': 0.02 * jax.random.normal(ks[7], (DEPTH, LRU_WIDTH), f32),
        'w_igate': nrm(ks[8], (DEPTH, LRU_BLOCKS, LRU_BLOCK_W, LRU_BLOCK_W), LRU_BLOCK_W),
        'b_igate': 0.02 * jax.random.normal(ks[9], (DEPTH, LRU_WIDTH), f32),
        'lru_lambda': lam,
        'w_br_lru': nrm(ks[11], (DEPTH, LRU_WIDTH, D_MODEL), LRU_WIDTH),
        'w_br_att': nrm(ks[12], (DEPTH, ATT_WIDTH, D_MODEL), ATT_WIDTH),
        'w_out': nrm(ks[13], (DEPTH, D_MODEL, D_MODEL), D_MODEL),
        'norm_mlp_g': gain(ks[14], (DEPTH, D_MODEL)),
        'w_mlp_up': nrm(ks[15], (DEPTH, D_MODEL, D_FF), D_MODEL),
        'w_mlp_down': nrm(ks[16], (DEPTH, D_FF, D_MODEL), D_FF),
        'norm_ple_g': gain(ks[17], (DEPTH, D_MODEL)),
        'w_ple_gate': nrm(ks[18], (DEPTH, D_MODEL, D_MODEL), D_MODEL),
        'w_ple': nrm(ks[19], (DEPTH, PLE_DIM, D_MODEL), PLE_DIM),
        'norm_final_g': gain(ks[20], (D_MODEL,)),
    }


def reference(x, p, norm_mix_g, w_in, conv_w, conv_b, w_rgate, b_rgate, w_igate, b_igate,
              lru_lambda, w_br_lru, w_br_att, w_out, norm_mlp_g, w_mlp_up, w_mlp_down,
              norm_ple_g, w_ple_gate, w_ple, norm_final_g):
    B, S, _ = x.shape
    split_idx = [int(s) for s in np.cumsum(IN_SPLITS)[:-1]]
    for l in range(DEPTH):
        h = rmsnorm(x, norm_mix_g[l])
        proj = h @ w_in[l]
        u_x, u_g, q, k, v, g_lru, g_att = jnp.split(proj, split_idx, axis=-1)

        c = causal_depthwise_conv(u_x, conv_w[l], conv_b[l])
        y_lru = rg_lru(c, w_rgate[l], b_rgate[l], w_igate[l], b_igate[l], lru_lambda[l]) * jax.nn.gelu(u_g)

        to_heads = lambda t: t.reshape(B, S, ATT_HEADS, ATT_HEAD_DIM).transpose(0, 2, 1, 3)
        y_att = stick_breaking_attention(to_heads(q), to_heads(k), to_heads(v))
        y_att = y_att.transpose(0, 2, 1, 3).reshape(B, S, ATT_WIDTH)

        merged = jax.nn.sigmoid(g_lru) * (y_lru @ w_br_lru[l]) + jax.nn.sigmoid(g_att) * (y_att @ w_br_att[l])
        x = x + merged @ w_out[l]

        h2 = rmsnorm(x, norm_mlp_g[l])
        x = x + jnp.square(jax.nn.relu(h2 @ w_mlp_up[l])) @ w_mlp_down[l]

        h3 = rmsnorm(x, norm_ple_g[l])
        x = x + jax.nn.sigmoid(h3 @ w_ple_gate[l]) * (p[l] @ w_ple[l])
    return rmsnorm(x, norm_final_g)
```

```python
import functools
import math

import jax
import jax.numpy as jnp
from jax import lax
from jax.experimental import pallas as pl
from jax.experimental.pallas import tpu as pltpu

F32 = jnp.float32
BF16 = jnp.bfloat16

NORM_EPS = 1e-6
LRU_C = 8.0
CONV_W = 4
HEAD_DIM = 64
LANES = 128
SUBLANES = 8
MXU_DIM = 256
HEADS_PER_BLOCK = LANES // HEAD_DIM

EXP_UNDERFLOW = -105.0

VMEM_LIMIT = 56 * 1024 * 1024


def _resident(shape):
    nd = len(shape)
    return pl.BlockSpec(shape, lambda *_: (0,) * nd, pipeline_mode=pl.Buffered(1))


def _rmsnorm(x, g):
    return x * lax.rsqrt(jnp.mean(x * x, axis=-1, keepdims=True) + NORM_EPS) * g


def _sigmoid(x):
    return 1.0 / (1.0 + jnp.exp(-x))


def _bdot(a, b):
    return jnp.dot(a, b, preferred_element_type=F32)


def _in_proj_kernel(x_ref, g_ref, w_ref, *out_refs):
    h = _rmsnorm(x_ref[...], g_ref[...]).astype(BF16)
    d = out_refs[0].shape[1]
    for n, o_ref in enumerate(out_refs):
        o_ref[...] = _bdot(h, w_ref[:, n * d:(n + 1) * d]).astype(o_ref.dtype)


def _in_proj(x2, g, w_bf, tm):
    t, d = x2.shape
    n_out = w_bf.shape[1] // d
    dtypes = [F32, F32, BF16, BF16, BF16, F32, F32]
    tile = pl.BlockSpec((tm, d), lambda i: (i, 0))
    return pl.pallas_call(
        _in_proj_kernel,
        grid=(t // tm,),
        in_specs=[tile, _resident((1, d)), _resident(w_bf.shape)],
        out_specs=[tile] * n_out,
        out_shape=[jax.ShapeDtypeStruct((t, d), dt) for dt in dtypes],
        compiler_params=pltpu.CompilerParams(
            dimension_semantics=("arbitrary",), vmem_limit_bytes=VMEM_LIMIT),
        name="in_proj",
    )(x2, g, w_bf)


def _lru_kernel(ux_ref, ug_ref, cw_ref, cb_ref, wr_ref, br_ref, wi_ref, bi_ref, lam_ref,
                y_ref, xbuf, a_scr, u_scr, h_scr, hprev):
    ts, w = ux_ref.shape

    @pl.when(pl.program_id(1) == 0)
    def _():
        xbuf[0:SUBLANES, :] = jnp.zeros((SUBLANES, w), F32)
        hprev[...] = jnp.zeros(hprev.shape, F32)

    xbuf[SUBLANES:, :] = ux_ref[...]
    c = cb_ref[...]
    for k in range(CONV_W):
        back = CONV_W - 1 - k
        c = c + cw_ref[k:k + 1, :] * xbuf[SUBLANES - back:SUBLANES - back + ts, :]
    xbuf[0:SUBLANES, :] = ux_ref[ts - SUBLANES:, :]

    cb = c.astype(BF16)
    gr, gi = [], []
    for j in range(w // MXU_DIM):
        sl = slice(j * MXU_DIM, (j + 1) * MXU_DIM)
        gr.append(_bdot(cb[:, sl], wr_ref[j]))
        gi.append(_bdot(cb[:, sl], wi_ref[j]))
    r = _sigmoid(jnp.concatenate(gr, axis=1) + br_ref[...])
    ig = _sigmoid(jnp.concatenate(gi, axis=1) + bi_ref[...])

    neg_lam = -lam_ref[...]
    softplus = jnp.maximum(neg_lam, 0.0) + jnp.log1p(jnp.exp(-jnp.abs(neg_lam)))
    log_a = (-LRU_C) * r * softplus
    a = jnp.exp(log_a)
    mult = jnp.sqrt(jnp.tanh(-log_a) * (a * a + 1.0))
    a_scr[...] = a
    u_scr[...] = mult * (ig * c)

    row = lax.broadcasted_iota(jnp.int32, (SUBLANES, w), 0)

    def group(gidx, h_in):
        r0 = pl.multiple_of(gidx * SUBLANES, SUBLANES)
        a8 = a_scr[pl.ds(r0, SUBLANES), :]
        u8 = u_scr[pl.ds(r0, SUBLANES), :]
        for dist in (1, 2, 4):
            keep = row >= dist
            u_sh = pltpu.roll(u8, dist, axis=0)
            a_sh = pltpu.roll(a8, dist, axis=0)
            u8 = jnp.where(keep, a8 * u_sh + u8, u8)
            a8 = jnp.where(keep, a8 * a_sh, a8)
        h8 = a8 * h_in + u8
        h_scr[pl.ds(r0, SUBLANES), :] = h8
        return h8[SUBLANES - 1:SUBLANES, :]

    hprev[...] = lax.fori_loop(0, ts // SUBLANES, group, hprev[...], unroll=2)

    y_ref[...] = (h_scr[...] * jax.nn.gelu(ug_ref[...], approximate=True)).astype(y_ref.dtype)


def _block_diag(wb, group):
    n, c, _ = wb.shape
    wg = wb.reshape(n // group, group, c, c)
    eye = jnp.eye(group, dtype=wb.dtype)
    dense = wg[:, :, :, None, :] * eye[None, :, None, :, None]
    return dense.reshape(n // group, group * c, group * c)


def _lru(ux, ug, conv_w, conv_b, w_r, b_r, w_i, b_i, lam, batch, ts):
    t, w = ux.shape
    nt = t // batch // ts
    group = MXU_DIM // w_r.shape[1]
    wr = _block_diag(w_r, group).astype(BF16)
    wi = _block_diag(w_i, group).astype(BF16)
    tile = pl.BlockSpec((ts, w), lambda b, i: (b * nt + i, 0))
    row = _resident((1, w))
    return pl.pallas_call(
        _lru_kernel,
        grid=(batch, nt),
        in_specs=[tile, tile, _resident(conv_w.shape), row, _resident(wr.shape), row,
                  _resident(wi.shape), row, row],
        out_specs=tile,
        out_shape=jax.ShapeDtypeStruct((t, w), BF16),
        scratch_shapes=[pltpu.VMEM((ts + SUBLANES, w), F32), pltpu.VMEM((ts, w), F32),
                        pltpu.VMEM((ts, w), F32), pltpu.VMEM((ts, w), F32),
                        pltpu.VMEM((1, w), F32)],
        compiler_params=pltpu.CompilerParams(
            dimension_semantics=("arbitrary", "arbitrary"), vmem_limit_bytes=VMEM_LIMIT),
        name="lru",
    )(ux, ug, conv_w, conv_b.reshape(1, w), wr, b_r.reshape(1, w), wi, b_i.reshape(1, w),
      lam.reshape(1, w))


def _attn_tile(q_heads, v_masks, k_tile, v_tile, carries, tri2, causal):
    out = None
    new_carries = []
    for qh, vm, carry in zip(q_heads, v_masks, carries):
        z = lax.dot_general(qh, k_tile, (((1,), (1,)), ((), ())), preferred_element_type=F32)
        soft = jnp.log(1.0 + jnp.exp(-jnp.abs(z)))
        log_1m = -(jnp.maximum(z, 0.0) + soft)
        log_b = jnp.minimum(z, 0.0) - soft
        if causal is not None:
            log_1m = jnp.where(causal, log_1m, 0.0)
        hi = log_1m.astype(BF16)
        lo = (log_1m - hi.astype(F32)).astype(BF16)
        within = _bdot(jnp.concatenate([hi, lo], axis=1), tri2)
        att = jnp.exp(log_b + within + carry)
        if causal is not None:
            att = jnp.where(causal, att, 0.0)
        pv = _bdot(att.astype(BF16), jnp.where(vm, v_tile, jnp.zeros_like(v_tile)))
        out = pv if out is None else out + pv
        new_carries.append(carry + within[:, 0:1] + log_1m[:, 0:1])
    return out, new_carries


def _attn_kernel(q_ref, k_ref, v_ref, o_ref):
    tq = q_ref.shape[0]
    tk = tq
    i = pl.program_id(2)

    lane = lax.broadcasted_iota(jnp.int32, (tq, LANES), 1)
    q = q_ref[...] * (HEAD_DIM ** -0.5)
    zero_q = jnp.zeros_like(q)
    q_heads = [jnp.where(lane // HEAD_DIM == h, q, zero_q) for h in range(HEADS_PER_BLOCK)]
    lane_k = lax.broadcasted_iota(jnp.int32, (tk, LANES), 1)
    v_masks = [lane_k // HEAD_DIM == h for h in range(HEADS_PER_BLOCK)]

    r2 = lax.broadcasted_iota(jnp.int32, (2 * tk, tk), 0)
    c2 = lax.broadcasted_iota(jnp.int32, (2 * tk, tk), 1)
    tri2 = ((r2 % tk) > c2).astype(BF16)
    rq = lax.broadcasted_iota(jnp.int32, (tq, tk), 0)
    ck = lax.broadcasted_iota(jnp.int32, (tq, tk), 1)
    causal = ck < rq

    def kv(j):
        r0 = pl.multiple_of(j * tk, tk)
        return k_ref[pl.ds(r0, tk), :], v_ref[pl.ds(r0, tk), :]

    zero_c = [jnp.zeros((tq, 1), F32)] * HEADS_PER_BLOCK
    acc, carries = _attn_tile(q_heads, v_masks, *kv(i), zero_c, tri2, causal)

    def live(carries):
        return functools.reduce(jnp.maximum, [jnp.max(c) for c in carries])

    def cond(state):
        j, top, _, _ = state
        return jnp.logical_and(j >= 0, top > EXP_UNDERFLOW)

    def body(state):
        j, _, acc, carries = state
        out, carries = _attn_tile(q_heads, v_masks, *kv(j), carries, tri2, None)
        return j - 1, live(carries), acc + out, carries

    _, _, acc, _ = lax.while_loop(cond, body, (i - 1, live(carries), acc, carries))
    o_ref[...] = acc.astype(o_ref.dtype)


def _attn(q, k, v, batch, tq):
    t, width = q.shape
    seq = t // batch
    nq = seq // tq
    q_spec = pl.BlockSpec((tq, LANES), lambda b, hp, i: (b * nq + i, hp))
    kv_spec = pl.BlockSpec((seq, LANES), lambda b, hp, i: (b, hp))
    return pl.pallas_call(
        _attn_kernel,
        grid=(batch, width // LANES, nq),
        in_specs=[q_spec, kv_spec, kv_spec],
        out_specs=q_spec,
        out_shape=jax.ShapeDtypeStruct((t, width), BF16),
        compiler_params=pltpu.CompilerParams(
            dimension_semantics=("arbitrary", "arbitrary", "arbitrary"),
            vmem_limit_bytes=VMEM_LIMIT),
        name="attn",
    )(q, k, v)


def _merge_kernel(x_ref, yl_ref, ya_ref, gl_ref, ga_ref, wbl_ref, wba_ref, wo_ref, o_ref):
    merged = (_sigmoid(gl_ref[...]) * _bdot(yl_ref[...], wbl_ref[...])
              + _sigmoid(ga_ref[...]) * _bdot(ya_ref[...], wba_ref[...]))
    o_ref[...] = x_ref[...] + _bdot(merged.astype(BF16), wo_ref[...])


def _merge(x2, y_lru, y_att, g_lru, g_att, w_bl, w_ba, w_o, tm):
    t, d = x2.shape
    tile = pl.BlockSpec((tm, d), lambda i: (i, 0))
    wspec = _resident((d, d))
    return pl.pallas_call(
        _merge_kernel,
        grid=(t // tm,),
        in_specs=[tile] * 5 + [wspec] * 3,
        out_specs=tile,
        out_shape=jax.ShapeDtypeStruct((t, d), F32),
        compiler_params=pltpu.CompilerParams(
            dimension_semantics=("arbitrary",), vmem_limit_bytes=VMEM_LIMIT),
        name="merge",
    )(x2, y_lru, y_att, g_lru, g_att, w_bl, w_ba, w_o)


def _mlp_ple_kernel(x_ref, p_ref, g_mlp_ref, wu_ref, wd_ref, g_ple_ref, wpg_ref, wpl_ref,
                    g_fin_ref, o_ref):
    x = x_ref[...]
    h2 = _rmsnorm(x, g_mlp_ref[...]).astype(BF16)
    up = jnp.square(jnp.maximum(_bdot(h2, wu_ref[...]), 0.0))
    x = x + _bdot(up.astype(BF16), wd_ref[...])
    h3 = _rmsnorm(x, g_ple_ref[...]).astype(BF16)
    gate = _sigmoid(_bdot(h3, wpg_ref[...]))
    x = x + gate * _bdot(p_ref[...].astype(BF16), wpl_ref[...])
    o_ref[...] = _rmsnorm(x, g_fin_ref[...])


def _mlp_ple(x1, p2, g_mlp, w_up, w_down, g_ple, w_pg, w_pl, g_fin, tm):
    t, d = x1.shape
    tile = pl.BlockSpec((tm, d), lambda i: (i, 0))
    ptile = pl.BlockSpec((tm, p2.shape[1]), lambda i: (i, 0))
    row = _resident((1, d))
    return pl.pallas_call(
        _mlp_ple_kernel,
        grid=(t // tm,),
        in_specs=[tile, ptile, row, _resident(w_up.shape), _resident(w_down.shape), row,
                  _resident(w_pg.shape), _resident(w_pl.shape), row],
        out_specs=tile,
        out_shape=jax.ShapeDtypeStruct((t, d), F32),
        compiler_params=pltpu.CompilerParams(
            dimension_semantics=("arbitrary",), vmem_limit_bytes=VMEM_LIMIT),
        name="mlp_ple",
    )(x1, p2, g_mlp, w_up, w_down, g_ple, w_pg, w_pl, g_fin)


def kernel(x, p, norm_mix_g, w_in, conv_w, conv_b, w_rgate, b_rgate, w_igate, b_igate, lru_lambda, w_br_lru, w_br_att, w_out, norm_mlp_g, w_mlp_up, w_mlp_down, norm_ple_g, w_ple_gate, w_ple, norm_final_g):
    batch, seq, d = x.shape
    depth = w_in.shape[0]
    t = batch * seq
    x2 = x.reshape(t, d)
    for l in range(depth):
        ux, ug, q, k, v, g_lru, g_att = _in_proj(
            x2, norm_mix_g[l].reshape(1, d), w_in[l].astype(BF16), tm=512)
        y_lru = _lru(ux, ug, conv_w[l], conv_b[l], w_rgate[l], b_rgate[l], w_igate[l],
                     b_igate[l], lru_lambda[l], batch, ts=512)
        y_att = _attn(q, k, v, batch, tq=256)
        x2 = _merge(x2, y_lru, y_att, g_lru, g_att, w_br_lru[l].astype(BF16),
                    w_br_att[l].astype(BF16), w_out[l].astype(BF16), tm=512)
        last = l == depth - 1
        assert last, "deeper stacks need a mlp_ple variant without the final norm"
        x2 = _mlp_ple(x2, p[l].reshape(t, -1), norm_mlp_g[l].reshape(1, d),
                      w_mlp_up[l].astype(BF16), w_mlp_down[l].astype(BF16),
                      norm_ple_g[l].reshape(1, d), w_ple_gate[l].astype(BF16),
                      w_ple[l].astype(BF16), norm_final_g.reshape(1, d), tm=256)
    return x2.reshape(batch, seq, d)
```

```python
import functools
import math

import jax
import jax.numpy as jnp
from jax import lax
from jax.experimental import pallas as pl
from jax.experimental.pallas import tpu as pltpu

F32 = jnp.float32
BF16 = jnp.bfloat16

NORM_EPS = 1e-6
LRU_C = 8.0
CONV_W = 4
HEAD_DIM = 64
LANES = 128
SUBLANES = 8
MXU_DIM = 256
HEADS_PER_BLOCK = LANES // HEAD_DIM

EXP_UNDERFLOW = 105.0
MASKED_LOGIT = -1e30
SIGN_BIT = 0x80000000
BF16_BITS_OF_F32 = 0xFFFF0000

VMEM_LIMIT = 56 * 1024 * 1024


def _resident(shape):
    nd = len(shape)
    return pl.BlockSpec(shape, lambda *_: (0,) * nd, pipeline_mode=pl.Buffered(1))


def _rmsnorm(x, g):
    return x * lax.rsqrt(jnp.mean(x * x, axis=-1, keepdims=True) + NORM_EPS) * g


def _sigmoid(x):
    return 1.0 / (1.0 + jnp.exp(-x))


def _bdot(a, b):
    return jnp.dot(a, b, preferred_element_type=F32)


def _in_proj_kernel(x_ref, g_ref, w_ref, *out_refs):
    h = _rmsnorm(x_ref[...], g_ref[...]).astype(BF16)
    d = out_refs[0].shape[1]
    for n, o_ref in enumerate(out_refs):
        o_ref[...] = _bdot(h, w_ref[:, n * d:(n + 1) * d]).astype(o_ref.dtype)


def _in_proj(x2, g, w_bf, tm):
    t, d = x2.shape
    n_out = w_bf.shape[1] // d
    dtypes = [F32, F32, BF16, BF16, BF16, F32, F32]
    tile = pl.BlockSpec((tm, d), lambda i: (i, 0))
    return pl.pallas_call(
        _in_proj_kernel,
        grid=(t // tm,),
        in_specs=[tile, _resident((1, d)), _resident(w_bf.shape)],
        out_specs=[tile] * n_out,
        out_shape=[jax.ShapeDtypeStruct((t, d), dt) for dt in dtypes],
        compiler_params=pltpu.CompilerParams(
            dimension_semantics=("arbitrary",), vmem_limit_bytes=VMEM_LIMIT),
        name="in_proj",
    )(x2, g, w_bf)


def _lru_kernel(ux_ref, ug_ref, cw_ref, cb_ref, wr_ref, br_ref, wi_ref, bi_ref, lam_ref,
                y_ref, xbuf, a_scr, u_scr, h_scr, hprev):
    ts, w = ux_ref.shape

    @pl.when(pl.program_id(1) == 0)
    def _():
        xbuf[0:SUBLANES, :] = jnp.zeros((SUBLANES, w), F32)
        hprev[...] = jnp.zeros(hprev.shape, F32)

    xbuf[SUBLANES:, :] = ux_ref[...]
    c = cb_ref[...]
    for k in range(CONV_W):
        back = CONV_W - 1 - k
        c = c + cw_ref[k:k + 1, :] * xbuf[SUBLANES - back:SUBLANES - back + ts, :]
    xbuf[0:SUBLANES, :] = ux_ref[ts - SUBLANES:, :]

    cb = c.astype(BF16)
    gr, gi = [], []
    for j in range(w // MXU_DIM):
        sl = slice(j * MXU_DIM, (j + 1) * MXU_DIM)
        gr.append(_bdot(cb[:, sl], wr_ref[j]))
        gi.append(_bdot(cb[:, sl], wi_ref[j]))
    r = _sigmoid(jnp.concatenate(gr, axis=1) + br_ref[...])
    ig = _sigmoid(jnp.concatenate(gi, axis=1) + bi_ref[...])

    neg_lam = -lam_ref[...]
    softplus = jnp.maximum(neg_lam, 0.0) + jnp.log1p(jnp.exp(-jnp.abs(neg_lam)))
    log_a = (-LRU_C) * r * softplus
    a = jnp.exp(log_a)
    mult = jnp.sqrt(jnp.tanh(-log_a) * (a * a + 1.0))
    a_scr[...] = a
    u_scr[...] = mult * (ig * c)

    row = lax.broadcasted_iota(jnp.int32, (SUBLANES, w), 0)

    def group(gidx, h_in):
        r0 = pl.multiple_of(gidx * SUBLANES, SUBLANES)
        a8 = a_scr[pl.ds(r0, SUBLANES), :]
        u8 = u_scr[pl.ds(r0, SUBLANES), :]
        for dist in (1, 2, 4):
            keep = row >= dist
            u_sh = pltpu.roll(u8, dist, axis=0)
            a_sh = pltpu.roll(a8, dist, axis=0)
            u8 = jnp.where(keep, a8 * u_sh + u8, u8)
            a8 = jnp.where(keep, a8 * a_sh, a8)
        h8 = a8 * h_in + u8
        h_scr[pl.ds(r0, SUBLANES), :] = h8
        return h8[SUBLANES - 1:SUBLANES, :]

    hprev[...] = lax.fori_loop(0, ts // SUBLANES, group, hprev[...], unroll=2)

    y_ref[...] = (h_scr[...] * jax.nn.gelu(ug_ref[...], approximate=True)).astype(y_ref.dtype)


def _block_diag(wb, group):
    n, c, _ = wb.shape
    wg = wb.reshape(n // group, group, c, c)
    eye = jnp.eye(group, dtype=wb.dtype)
    dense = wg[:, :, :, None, :] * eye[None, :, None, :, None]
    return dense.reshape(n // group, group * c, group * c)


def _lru(ux, ug, conv_w, conv_b, w_r, b_r, w_i, b_i, lam, batch, ts):
    t, w = ux.shape
    nt = t // batch // ts
    group = MXU_DIM // w_r.shape[1]
    wr = _block_diag(w_r, group).astype(BF16)
    wi = _block_diag(w_i, group).astype(BF16)
    tile = pl.BlockSpec((ts, w), lambda b, i: (b * nt + i, 0))
    row = _resident((1, w))
    return pl.pallas_call(
        _lru_kernel,
        grid=(batch, nt),
        in_specs=[tile, tile, _resident(conv_w.shape), row, _resident(wr.shape), row,
                  _resident(wi.shape), row, row],
        out_specs=tile,
        out_shape=jax.ShapeDtypeStruct((t, w), BF16),
        scratch_shapes=[pltpu.VMEM((ts + SUBLANES, w), F32), pltpu.VMEM((ts, w), F32),
                        pltpu.VMEM((ts, w), F32), pltpu.VMEM((ts, w), F32),
                        pltpu.VMEM((1, w), F32)],
        compiler_params=pltpu.CompilerParams(
            dimension_semantics=("arbitrary", "arbitrary"), vmem_limit_bytes=VMEM_LIMIT),
        name="lru",
    )(ux, ug, conv_w, conv_b.reshape(1, w), wr, b_r.reshape(1, w), wi, b_i.reshape(1, w),
      lam.reshape(1, w))


def _bits(x):
    return lax.bitcast_convert_type(x, jnp.uint32)


def _f32_of_bits(x):
    return lax.bitcast_convert_type(x, F32)


def _attn_tile(q_heads, v_masks, k_tile, v_tile, carries, tri2, causal):
    out = None
    new_carries = []
    for qh, vm, carry in zip(q_heads, v_masks, carries):
        z = lax.dot_general(qh, k_tile, (((1,), (1,)), ((), ())), preferred_element_type=F32)
        if causal is not None:
            z = jnp.where(causal, z, MASKED_LOGIT)
        neg_abs = _f32_of_bits(_bits(z) | jnp.uint32(SIGN_BIT))
        pos = jnp.maximum(z, 0.0) + jnp.log(1.0 + jnp.exp(neg_abs))
        hi32 = _f32_of_bits(_bits(pos) & jnp.uint32(BF16_BITS_OF_F32))
        within = _bdot(jnp.concatenate([hi32.astype(BF16), (pos - hi32).astype(BF16)], axis=1),
                       tri2)
        att = jnp.exp(z - (pos + within + carry))
        pv = _bdot(att.astype(BF16), jnp.where(vm, v_tile, jnp.zeros_like(v_tile)))
        out = pv if out is None else out + pv
        new_carries.append(carry + within[:, 0:1] + pos[:, 0:1])
    return out, new_carries


def _attn_kernel(q_ref, k_ref, v_ref, tri_ref, o_ref):
    tq = q_ref.shape[0]
    tk = tri_ref.shape[1]
    i = pl.program_id(2)

    lane = lax.broadcasted_iota(jnp.int32, (tq, LANES), 1)
    q = q_ref[...] * (HEAD_DIM ** -0.5)
    zero_q = jnp.zeros_like(q)
    q_heads = [jnp.where(lane // HEAD_DIM == h, q, zero_q) for h in range(HEADS_PER_BLOCK)]
    lane_k = lax.broadcasted_iota(jnp.int32, (tk, LANES), 1)
    v_masks = [lane_k // HEAD_DIM == h for h in range(HEADS_PER_BLOCK)]

    tri2 = tri_ref[...]
    rq =lax.broadcasted_iota(jnp.int32, (tq, tk), 0)
    ck = lax.broadcasted_iota(jnp.int32, (tq, tk), 1)
    causal = ck < rq

    def kv(j):
        r0 = pl.multiple_of(j * tk, tk)
        return k_ref[pl.ds(r0, tk), :], v_ref[pl.ds(r0, tk), :]

    zero_c = [jnp.zeros((tq, 1), F32)] * HEADS_PER_BLOCK
    acc, carries = _attn_tile(q_heads, v_masks, *kv(i), zero_c, tri2, causal)
    prev, carries = _attn_tile(q_heads, v_masks, *kv(jnp.maximum(i - 1, 0)), carries, tri2, None)
    acc = acc + jnp.where(i > 0, prev, jnp.zeros_like(prev))

    def decayed(carries):
        return functools.reduce(jnp.minimum, [jnp.min(c) for c in carries])

    def cond(state):
        j, low, _, _ = state
        return jnp.logical_and(j >= 0, low < EXP_UNDERFLOW)

    def body(state):
        j, _, acc, carries = state
        out, carries = _attn_tile(q_heads, v_masks, *kv(j), carries, tri2, None)
        return j - 1, decayed(carries), acc + out, carries

    _, _, acc, _ = lax.while_loop(cond, body, (i - 2, decayed(carries), acc, carries))
    o_ref[...] = acc.astype(o_ref.dtype)


def _attn(q, k, v, batch, tq):
    t, width = q.shape
    seq = t // batch
    nq = seq // tq
    tk = tq
    later = (jnp.arange(2 * tk)[:, None] % tk) > jnp.arange(tk)[None, :]
    q_spec = pl.BlockSpec((tq, LANES), lambda b, hp, i: (b * nq + i, hp))
    kv_spec = pl.BlockSpec((seq, LANES), lambda b, hp, i: (b, hp))
    return pl.pallas_call(
        _attn_kernel,
        grid=(batch, width // LANES, nq),
        in_specs=[q_spec, kv_spec, kv_spec, _resident((2 * tk, tk))],
        out_specs=q_spec,
        out_shape=jax.ShapeDtypeStruct((t, width), BF16),
        compiler_params=pltpu.CompilerParams(
            dimension_semantics=("arbitrary", "arbitrary", "arbitrary"),
            vmem_limit_bytes=VMEM_LIMIT),
        name="attn",
    )(q, k, v, later.astype(BF16))


def _merge_kernel(x_ref, yl_ref, ya_ref, gl_ref, ga_ref, wbl_ref, wba_ref, wo_ref, o_ref):
    merged = (_sigmoid(gl_ref[...]) * _bdot(yl_ref[...], wbl_ref[...])
              + _sigmoid(ga_ref[...]) * _bdot(ya_ref[...], wba_ref[...]))
    o_ref[...] = x_ref[...] + _bdot(merged.astype(BF16), wo_ref[...])


def _merge(x2, y_lru, y_att, g_lru, g_att, w_bl, w_ba, w_o, tm):
    t, d = x2.shape
    tile = pl.BlockSpec((tm, d), lambda i: (i, 0))
    wspec = _resident((d, d))
    return pl.pallas_call(
        _merge_kernel,
        grid=(t // tm,),
        in_specs=[tile] * 5 + [wspec] * 3,
        out_specs=tile,
        out_shape=jax.ShapeDtypeStruct((t, d), F32),
        compiler_params=pltpu.CompilerParams(
            dimension_semantics=("arbitrary",), vmem_limit_bytes=VMEM_LIMIT),
        name="merge",
    )(x2, y_lru, y_att, g_lru, g_att, w_bl, w_ba, w_o)


def _mlp_ple_kernel(x_ref, p_ref, g_mlp_ref, wu_ref, wd_ref, g_ple_ref, wpg_ref, wpl_ref,
                    g_fin_ref, o_ref):
    x = x_ref[...]
    h2 = _rmsnorm(x, g_mlp_ref[...]).astype(BF16)
    up = jnp.square(jnp.maximum(_bdot(h2, wu_ref[...]), 0.0))
    x = x + _bdot(up.astype(BF16), wd_ref[...])
    h3 = _rmsnorm(x, g_ple_ref[...]).astype(BF16)
    gate = _sigmoid(_bdot(h3, wpg_ref[...]))
    x = x + gate * _bdot(p_ref[...].astype(BF16), wpl_ref[...])
    o_ref[...] = _rmsnorm(x, g_fin_ref[...])


def _mlp_ple(x1, p2, g_mlp, w_up, w_down, g_ple, w_pg, w_pl, g_fin, tm):
    t, d = x1.shape
    tile = pl.BlockSpec((tm, d), lambda i: (i, 0))
    ptile = pl.BlockSpec((tm, p2.shape[1]), lambda i: (i, 0))
    row = _resident((1, d))
    return pl.pallas_call(
        _mlp_ple_kernel,
        grid=(t // tm,),
        in_specs=[tile, ptile, row, _resident(w_up.shape), _resident(w_down.shape), row,
                  _resident(w_pg.shape), _resident(w_pl.shape), row],
        out_specs=tile,
        out_shape=jax.ShapeDtypeStruct((t, d), F32),
        compiler_params=pltpu.CompilerParams(
            dimension_semantics=("arbitrary",), vmem_limit_bytes=VMEM_LIMIT),
        name="mlp_ple",
    )(x1, p2, g_mlp, w_up, w_down, g_ple, w_pg, w_pl, g_fin)


def kernel(x, p, norm_mix_g, w_in, conv_w, conv_b, w_rgate, b_rgate, w_igate, b_igate, lru_lambda, w_br_lru, w_br_att, w_out, norm_mlp_g, w_mlp_up, w_mlp_down, norm_ple_g, w_ple_gate, w_ple, norm_final_g):
    batch, seq, d = x.shape
    depth = w_in.shape[0]
    t = batch * seq
    x2 = x.reshape(t, d)
    for l in range(depth):
        ux, ug, q, k, v, g_lru, g_att = _in_proj(
            x2, norm_mix_g[l].reshape(1, d), w_in[l].astype(BF16), tm=512)
        y_lru = _lru(ux, ug, conv_w[l], conv_b[l], w_rgate[l], b_rgate[l], w_igate[l],
                     b_igate[l], lru_lambda[l], batch, ts=512)
        y_att = _attn(q, k, v, batch, tq=256)
        x2 = _merge(x2, y_lru, y_att, g_lru, g_att, w_br_lru[l].astype(BF16),
                    w_br_att[l].astype(BF16), w_out[l].astype(BF16), tm=512)
        last = l == depth - 1
        assert last, "deeper stacks need a mlp_ple variant without the final norm"
        x2 = _mlp_ple(x2, p[l].reshape(t, -1), norm_mlp_g[l].reshape(1, d),
                      w_mlp_up[l].astype(BF16), w_mlp_down[l].astype(BF16),
                      norm_ple_g[l].reshape(1, d), w_ple_gate[l].astype(BF16),
                      w_ple[l].astype(BF16), norm_final_g.reshape(1, d), tm=256)
    return x2.reshape(batch, seq, d)
```

```python
import functools

import jax
import jax.numpy as jnp
from jax import lax
from jax.experimental import pallas as pl
from jax.experimental.pallas import tpu as pltpu

F32 = jnp.float32
BF16 = jnp.bfloat16

NORM_EPS = 1e-6
LRU_C = 8.0
CONV_W = 4
HEAD_DIM = 64
LANES = 128
SUBLANES = 8
MXU_DIM = 256
HEADS_PER_BLOCK = LANES // HEAD_DIM

EXP_UNDERFLOW = 105.0
MASKED_LOGIT = -1e30

VMEM_LIMIT = 56 * 1024 * 1024


def _resident(shape):
    nd = len(shape)
    return pl.BlockSpec(shape, lambda *_: (0,) * nd, pipeline_mode=pl.Buffered(1))


def _rmsnorm(x, g):
    return x * lax.rsqrt(jnp.mean(x * x, axis=-1, keepdims=True) + NORM_EPS) * g


def _sigmoid(x):
    return 0.5 * jnp.tanh(0.5 * x) + 0.5


def _bdot(a, b):
    return jnp.dot(a, b, preferred_element_type=F32)


def _shift_rows(cur, prev_tail, back, row):
    rolled = pltpu.roll(cur, back, axis=0)
    first = jnp.where(row < back, pltpu.roll(prev_tail, back, axis=0), rolled[0:SUBLANES])
    return jnp.concatenate([first, rolled[SUBLANES:]], axis=0)


def _in_lru_kernel(x_ref, g_ref, w_ref, cw_ref, cb_ref, wr_ref, br_ref, wi_ref, bi_ref, lam_ref,
                   q_ref, k_ref, v_ref, gl_ref, ga_ref, y_ref,
                   tail, a_scr, u_scr, h_scr, hprev):
    ts, d = x_ref.shape

    @pl.when(pl.program_id(1) == 0)
    def _():
        tail[...] = jnp.zeros(tail.shape, F32)
        hprev[...] = jnp.zeros(hprev.shape, F32)

    h = _rmsnorm(x_ref[...], g_ref[...]).astype(BF16)

    def proj(n):
        return _bdot(h, w_ref[:, n * d:(n + 1) * d])

    outs = {2: q_ref, 3: k_ref, 4: v_ref, 5: gl_ref, 6: ga_ref}

    def emit(n):
        outs[n][...] = proj(n).astype(outs[n].dtype)

    ux = proj(0)
    ug = proj(1)
    row = lax.broadcasted_iota(jnp.int32, (SUBLANES, d), 0)
    prev_tail = tail[...]
    c = cb_ref[...]
    for k in range(CONV_W):
        back = CONV_W - 1 - k
        xs = ux if back == 0 else _shift_rows(ux, prev_tail, back, row)
        c = c + cw_ref[k:k + 1, :] * xs
    tail[...] = ux[ts - SUBLANES:, :]
    emit(2)

    cb = c.astype(BF16)
    gr, gi = [], []
    for j in range(d // MXU_DIM):
        sl = slice(j * MXU_DIM, (j + 1) * MXU_DIM)
        gr.append(_bdot(cb[:, sl], wr_ref[j]))
        gi.append(_bdot(cb[:, sl], wi_ref[j]))
    r = _sigmoid(jnp.concatenate(gr, axis=1) + br_ref[...])
    ig = _sigmoid(jnp.concatenate(gi, axis=1) + bi_ref[...])
    emit(3)

    neg_lam = -lam_ref[...]
    softplus = jnp.maximum(neg_lam, 0.0) + jnp.log1p(jnp.exp(-jnp.abs(neg_lam)))
    log_a = (-LRU_C) * r * softplus
    a = jnp.exp(log_a)
    one_m_a2 = jnp.tanh(-log_a) * (a * a + 1.0)
    mult = jnp.where(one_m_a2 > 0.0, one_m_a2 * lax.rsqrt(one_m_a2), 0.0)
    a_scr[...] = a
    u_scr[...] = mult * (ig * c)
    emit(4)

    h_in = hprev[...]
    for g0 in range(0, ts, SUBLANES):
        if g0 == ts // 2:
            emit(5)
        a8 = a_scr[g0:g0 + SUBLANES, :]
        u8 = u_scr[g0:g0 + SUBLANES, :]
        for dist in (1, 2, 4):
            keep = row >= dist
            u_sh = pltpu.roll(u8, dist, axis=0)
            a_sh = pltpu.roll(a8, dist, axis=0)
            u8 = jnp.where(keep, a8 * u_sh + u8, u8)
            a8 = jnp.where(keep, a8 * a_sh, a8)
        h8 = a8 * h_in + u8
        h_scr[g0:g0 + SUBLANES, :] = h8
        h_in = h8[SUBLANES - 1:SUBLANES, :]
    hprev[...] = h_in
    emit(6)

    y_ref[...] = (h_scr[...] * jax.nn.gelu(ug, approximate=True)).astype(y_ref.dtype)


def _block_diag(wb, group):
    n, c, _ = wb.shape
    wg = wb.reshape(n // group, group, c, c)
    eye = jnp.eye(group, dtype=wb.dtype)
    dense = wg[:, :, :, None, :] * eye[None, :, None, :, None]
    return dense.reshape(n // group, group * c, group * c)


def _in_lru(x2, g, w_bf, conv_w, conv_b, w_r, b_r, w_i, b_i, lam, batch, ts):
    t, d = x2.shape
    nt = t // batch // ts
    group = MXU_DIM // w_r.shape[1]
    wr = _block_diag(w_r, group).astype(BF16)
    wi = _block_diag(w_i, group).astype(BF16)
    tile = pl.BlockSpec((ts, d), lambda b, i: (b * nt + i, 0))
    row = _resident((1, d))
    dtypes = [BF16, BF16, BF16, F32, F32, BF16]
    return pl.pallas_call(
        _in_lru_kernel,
        grid=(batch, nt),
        in_specs=[tile, row, _resident(w_bf.shape), _resident(conv_w.shape), row,
                  _resident(wr.shape), row, _resident(wi.shape), row, row],
        out_specs=[tile] * len(dtypes),
        out_shape=[jax.ShapeDtypeStruct((t, d), dt) for dt in dtypes],
        scratch_shapes=[pltpu.VMEM((SUBLANES, d), F32), pltpu.VMEM((ts, d), F32),
                        pltpu.VMEM((ts, d), F32), pltpu.VMEM((ts, d), F32),
                        pltpu.VMEM((1, d), F32)],
        compiler_params=pltpu.CompilerParams(
            dimension_semantics=("arbitrary", "arbitrary"), vmem_limit_bytes=VMEM_LIMIT),
        name="in_lru",
    )(x2, g, w_bf, conv_w, conv_b.reshape(1, d), wr, b_r.reshape(1, d), wi, b_i.reshape(1, d),
      lam.reshape(1, d))


def _attn_tile(q_heads, v_masks, k_tile, v_tile, carries, tri2, causal):
    out = None
    new_carries = []
    for qh, vm, carry in zip(q_heads, v_masks, carries):
        z = lax.dot_general(qh, k_tile, (((1,), (1,)), ((), ())), preferred_element_type=F32)
        if causal is not None:
            z = jnp.where(causal, z, MASKED_LOGIT)
        pos = jnp.maximum(z, 0.0) + jnp.log(1.0 + jnp.exp(-jnp.abs(z)))
        hi = pos.astype(BF16)
        lo = (pos - hi.astype(F32)).astype(BF16)
        within = _bdot(jnp.concatenate([hi, lo], axis=1), tri2)
        att = jnp.exp(z - (pos + within + carry))
        pv = _bdot(att.astype(BF16), jnp.where(vm, v_tile, jnp.zeros_like(v_tile)))
        out = pv if out is None else out + pv
        new_carries.append(carry + within[:, 0:1] + pos[:, 0:1])
    return out, new_carries


def _attn_kernel(q_ref, k_ref, v_ref, tri_ref, o_ref):
    tq = q_ref.shape[0]
    tk = tri_ref.shape[1]
    i = pl.program_id(2)

    lane = lax.broadcasted_iota(jnp.int32, (tq, LANES), 1)
    q = q_ref[...] * (HEAD_DIM ** -0.5)
    zero_q = jnp.zeros_like(q)
    q_heads = [jnp.where(lane // HEAD_DIM == h, q, zero_q) for h in range(HEADS_PER_BLOCK)]
    lane_k = lax.broadcasted_iota(jnp.int32, (tk, LANES), 1)
    v_masks = [lane_k // HEAD_DIM == h for h in range(HEADS_PER_BLOCK)]
    tri2 = tri_ref[...]
    rq = lax.broadcasted_iota(jnp.int32, (tq, tk), 0)
    ck = lax.broadcasted_iota(jnp.int32, (tq, tk), 1)
    causal = ck < rq

    def kv(j):
        r0 = pl.multiple_of(j * tk, tk)
        return k_ref[pl.ds(r0, tk), :], v_ref[pl.ds(r0, tk), :]

    zero_c = [jnp.zeros((tq, 1), F32)] * HEADS_PER_BLOCK
    acc, carries = _attn_tile(q_heads, v_masks, *kv(i), zero_c, tri2, causal)
    prev, carries = _attn_tile(q_heads, v_masks, *kv(jnp.maximum(i - 1, 0)), carries, tri2, None)
    acc = acc + jnp.where(i > 0, prev, jnp.zeros_like(prev))

    def decayed(carries):
        return functools.reduce(jnp.minimum, [jnp.min(c) for c in carries])

    def cond(state):
        j, low, _, _ = state
        return jnp.logical_and(j >= 0, low < EXP_UNDERFLOW)

    def body(state):
        j, _, acc, carries = state
        out, carries = _attn_tile(q_heads, v_masks, *kv(j), carries, tri2, None)
        return j - 1, decayed(carries), acc + out, carries

    _, _, acc, _ = lax.while_loop(cond, body, (i - 2, decayed(carries), acc, carries))
    o_ref[...] = acc.astype(o_ref.dtype)


def _attn(q, k, v, batch, tq):
    t, width = q.shape
    seq = t // batch
    nq = seq // tq
    tk = tq
    later = (jnp.arange(2 * tk)[:, None] % tk) > jnp.arange(tk)[None, :]
    q_spec = pl.BlockSpec((tq, LANES), lambda b, hp, i: (b * nq + i, hp))
    kv_spec = pl.BlockSpec((seq, LANES), lambda b, hp, i: (b, hp))
    return pl.pallas_call(
        _attn_kernel,
        grid=(batch, width // LANES, nq),
        in_specs=[q_spec, kv_spec, kv_spec, _resident((2 * tk, tk))],
        out_specs=q_spec,
        out_shape=jax.ShapeDtypeStruct((t, width), BF16),
        compiler_params=pltpu.CompilerParams(
            dimension_semantics=("arbitrary", "arbitrary", "arbitrary"),
            vmem_limit_bytes=VMEM_LIMIT),
        name="attn",
    )(q, k, v, later.astype(BF16))


def _merge_kernel(x_ref, yl_ref, ya_ref, gl_ref, ga_ref, wbl_ref, wba_ref, wo_ref, o_ref):
    merged = (_sigmoid(gl_ref[...]) * _bdot(yl_ref[...], wbl_ref[...])
              + _sigmoid(ga_ref[...]) * _bdot(ya_ref[...], wba_ref[...]))
    o_ref[...] = x_ref[...] + _bdot(merged.astype(BF16), wo_ref[...])


def _merge(x2, y_lru, y_att, g_lru, g_att, w_bl, w_ba, w_o, tm):
    t, d = x2.shape
    tile = pl.BlockSpec((tm, d), lambda i: (i, 0))
    wspec = _resident((d, d))
    return pl.pallas_call(
        _merge_kernel,
        grid=(t // tm,),
        in_specs=[tile] * 5 + [wspec] * 3,
        out_specs=tile,
        out_shape=jax.ShapeDtypeStruct((t, d), F32),
        compiler_params=pltpu.CompilerParams(
            dimension_semantics=("arbitrary",), vmem_limit_bytes=VMEM_LIMIT),
        name="merge",
    )(x2, y_lru, y_att, g_lru, g_att, w_bl, w_ba, w_o)


def _mlp_ple_kernel(x_ref, p_ref, g_mlp_ref, wu_ref, wd_ref, g_ple_ref, wpg_ref, wpl_ref,
                    g_fin_ref, o_ref):
    x = x_ref[...]
    h2 = _rmsnorm(x, g_mlp_ref[...]).astype(BF16)
    up = jnp.square(jnp.maximum(_bdot(h2, wu_ref[...]), 0.0))
    x = x + _bdot(up.astype(BF16), wd_ref[...])
    h3 = _rmsnorm(x, g_ple_ref[...]).astype(BF16)
    gate = _sigmoid(_bdot(h3, wpg_ref[...]))
    x = x + gate * _bdot(p_ref[...].astype(BF16), wpl_ref[...])
    o_ref[...] = _rmsnorm(x, g_fin_ref[...])


def _mlp_ple(x1, p2, g_mlp, w_up, w_down, g_ple, w_pg, w_pl, g_fin, tm):
    t, d = x1.shape
    tile = pl.BlockSpec((tm, d), lambda i: (i, 0))
    ptile = pl.BlockSpec((tm, p2.shape[1]), lambda i: (i, 0))
    row = _resident((1, d))
    return pl.pallas_call(
        _mlp_ple_kernel,
        grid=(t // tm,),
        in_specs=[tile, ptile, row, _resident(w_up.shape), _resident(w_down.shape), row,
                  _resident(w_pg.shape), _resident(w_pl.shape), row],
        out_specs=tile,
        out_shape=jax.ShapeDtypeStruct((t, d), F32),
        compiler_params=pltpu.CompilerParams(
            dimension_semantics=("arbitrary",), vmem_limit_bytes=VMEM_LIMIT),
        name="mlp_ple",
    )(x1, p2, g_mlp, w_up, w_down, g_ple, w_pg, w_pl, g_fin)


def kernel(x, p, norm_mix_g, w_in, conv_w, conv_b, w_rgate, b_rgate, w_igate, b_igate, lru_lambda, w_br_lru, w_br_att, w_out, norm_mlp_g, w_mlp_up, w_mlp_down, norm_ple_g, w_ple_gate, w_ple, norm_final_g):
    batch, seq, d = x.shape
    depth = w_in.shape[0]
    t = batch * seq
    x2 = x.reshape(t, d)
    for l in range(depth):
        q, k, v, g_lru, g_att, y_lru = _in_lru(
            x2, norm_mix_g[l].reshape(1, d), w_in[l].astype(BF16), conv_w[l], conv_b[l],
            w_rgate[l], b_rgate[l], w_igate[l], b_igate[l], lru_lambda[l], batch, ts=512)
        y_att = _attn(q, k, v, batch, tq=256)
        x2 = _merge(x2, y_lru, y_att, g_lru, g_att, w_br_lru[l].astype(BF16),
                    w_br_att[l].astype(BF16), w_out[l].astype(BF16), tm=512)
        last = l == depth - 1
        assert last, "deeper stacks need a mlp_ple variant without the final norm"
        x2 = _mlp_ple(x2, p[l].reshape(t, -1), norm_mlp_g[l].reshape(1, d),
                      w_mlp_up[l].astype(BF16), w_mlp_down[l].astype(BF16),
                      norm_ple_g[l].reshape(1, d), w_ple_gate[l].astype(BF16),
                      w_ple[l].astype(BF16), norm_final_g.reshape(1, d), tm=256)
    return x2.reshape(batch, seq, d)
```

```python
import functools

import jax
import jax.numpy as jnp
from jax import lax
from jax.experimental import pallas as pl
from jax.experimental.pallas import tpu as pltpu

F32 = jnp.float32
BF16 = jnp.bfloat16

NORM_EPS = 1e-6
LRU_C = 8.0
CONV_W = 4
HEAD_DIM = 64
LANES = 128
SUBLANES = 8
MXU_DIM = 256
HEADS_PER_BLOCK = LANES // HEAD_DIM

EXP_UNDERFLOW = 105.0
MASKED_LOGIT = -1e30

VMEM_LIMIT = 56 * 1024 * 1024


def _resident(shape):
    nd = len(shape)
    return pl.BlockSpec(shape, lambda *_: (0,) * nd, pipeline_mode=pl.Buffered(1))


def _rmsnorm(x, g):
    return x * lax.rsqrt(jnp.mean(x * x, axis=-1, keepdims=True) + NORM_EPS) * g


def _sigmoid(x):
    return 0.5 * jnp.tanh(0.5 * x) + 0.5


def _bdot(a, b):
    return jnp.dot(a, b, preferred_element_type=F32)


def _shift_rows(cur, prev_tail, back, row):
    rolled = pltpu.roll(cur, back, axis=0)
    first = jnp.where(row < back, pltpu.roll(prev_tail, back, axis=0), rolled[0:SUBLANES])
    return jnp.concatenate([first, rolled[SUBLANES:]], axis=0)


def _in_lru_kernel(x_ref, g_ref, w_ref, cw_ref, cb_ref, wr_ref, br_ref, wi_ref, bi_ref, lam_ref,
                   q_ref, k_ref, v_ref, gl_ref, ga_ref, y_ref,
                   tail, a_scr, u_scr, h_scr, hprev):
    ts, d = x_ref.shape

    @pl.when(pl.program_id(1) == 0)
    def _():
        tail[...] = jnp.zeros(tail.shape, F32)
        hprev[...] = jnp.zeros(hprev.shape, F32)

    h = _rmsnorm(x_ref[...], g_ref[...]).astype(BF16)

    def proj(n):
        return _bdot(h, w_ref[:, n * d:(n + 1) * d])

    outs = {2: q_ref, 3: k_ref, 4: v_ref, 5: gl_ref, 6: ga_ref}

    def emit(n):
        outs[n][...] = proj(n).astype(outs[n].dtype)

    ux = proj(0)
    ug = proj(1)
    row = lax.broadcasted_iota(jnp.int32, (SUBLANES, d), 0)
    prev_tail = tail[...]
    c = cb_ref[...]
    for k in range(CONV_W):
        back = CONV_W - 1 - k
        xs = ux if back == 0 else _shift_rows(ux, prev_tail, back, row)
        c = c + cw_ref[k:k + 1, :] * xs
    tail[...] = ux[ts - SUBLANES:, :]
    emit(2)

    cb = c.astype(BF16)
    gr, gi = [], []
    for j in range(d // MXU_DIM):
        sl = slice(j * MXU_DIM, (j + 1) * MXU_DIM)
        gr.append(_bdot(cb[:, sl], wr_ref[j]))
        gi.append(_bdot(cb[:, sl], wi_ref[j]))
    r = _sigmoid(jnp.concatenate(gr, axis=1) + br_ref[...])
    ig = _sigmoid(jnp.concatenate(gi, axis=1) + bi_ref[...])
    emit(3)

    neg_lam = -lam_ref[...]
    softplus = jnp.maximum(neg_lam, 0.0) + jnp.log1p(jnp.exp(-jnp.abs(neg_lam)))
    log_a = (-LRU_C) * r * softplus
    a = jnp.exp(log_a)
    one_m_a2 = jnp.tanh(-log_a) * (a * a + 1.0)
    mult = jnp.where(one_m_a2 > 0.0, one_m_a2 * lax.rsqrt(one_m_a2), 0.0)
    a_scr[...] = a
    u_scr[...] = mult * (ig * c)
    emit(4)

    h_in = hprev[...]
    for g0 in range(0, ts, SUBLANES):
        if g0 == ts // 2:
            emit(5)
        a8 = a_scr[g0:g0 + SUBLANES, :]
        u8 = u_scr[g0:g0 + SUBLANES, :]
        for dist in (1, 2, 4):
            keep = row >= dist
            u_sh = pltpu.roll(u8, dist, axis=0)
            a_sh = pltpu.roll(a8, dist, axis=0)
            u8 = jnp.where(keep, a8 * u_sh + u8, u8)
            a8 = jnp.where(keep, a8 * a_sh, a8)
        h8 = a8 * h_in + u8
        h_scr[g0:g0 + SUBLANES, :] = h8
        h_in = h8[SUBLANES - 1:SUBLANES, :]
    hprev[...] = h_in
    emit(6)

    y_ref[...] = (h_scr[...] * jax.nn.gelu(ug, approximate=True)).astype(y_ref.dtype)


def _block_diag(wb, group):
    n, c, _ = wb.shape
    wg = wb.reshape(n // group, group, c, c)
    eye = jnp.eye(group, dtype=wb.dtype)
    dense = wg[:, :, :, None, :] * eye[None, :, None, :, None]
    return dense.reshape(n // group, group * c, group * c)


def _in_lru(x2, g, w_bf, conv_w, conv_b, w_r, b_r, w_i, b_i, lam, batch, ts):
    t, d = x2.shape
    nt = t // batch // ts
    group = MXU_DIM // w_r.shape[1]
    wr = _block_diag(w_r, group).astype(BF16)
    wi = _block_diag(w_i, group).astype(BF16)
    tile = pl.BlockSpec((ts, d), lambda b, i: (b * nt + i, 0))
    row = _resident((1, d))
    dtypes = [BF16, BF16, BF16, F32, F32, BF16]
    return pl.pallas_call(
        _in_lru_kernel,
        grid=(batch, nt),
        in_specs=[tile, row, _resident(w_bf.shape), _resident(conv_w.shape), row,
                  _resident(wr.shape), row, _resident(wi.shape), row, row],
        out_specs=[tile] * len(dtypes),
        out_shape=[jax.ShapeDtypeStruct((t, d), dt) for dt in dtypes],
        scratch_shapes=[pltpu.VMEM((SUBLANES, d), F32), pltpu.VMEM((ts, d), F32),
                        pltpu.VMEM((ts, d), F32), pltpu.VMEM((ts, d), F32),
                        pltpu.VMEM((1, d), F32)],
        compiler_params=pltpu.CompilerParams(
            dimension_semantics=("arbitrary", "arbitrary"), vmem_limit_bytes=VMEM_LIMIT),
        name="in_lru",
    )(x2, g, w_bf, conv_w, conv_b.reshape(1, d), wr, b_r.reshape(1, d), wi, b_i.reshape(1, d),
      lam.reshape(1, d))


def _softplus(z):
    return jnp.maximum(z, 0.0) + jnp.log(1.0 + jnp.exp(-jnp.abs(z)))


def _hi_lo(pos):
    hi = pos.astype(BF16)
    lo = (pos - hi.astype(F32)).astype(BF16)
    return jnp.concatenate([hi, lo], axis=1)


def _attn_kernel(q_ref, k_ref, v_ref, tri_ref, o_ref):
    tq = q_ref.shape[0]
    tk = tri_ref.shape[1]
    i = pl.program_id(2)

    lane = lax.broadcasted_iota(jnp.int32, (tq, LANES), 1)
    q = q_ref[...] * (HEAD_DIM ** -0.5)
    zero_q = jnp.zeros_like(q)
    q2 = jnp.concatenate(
        [jnp.where(lane // HEAD_DIM == h, q, zero_q) for h in range(HEADS_PER_BLOCK)], axis=0)
    tri2 = tri_ref[...]

    def scores(k_tiles):
        return lax.dot_general(q2, k_tiles, (((1,), (1,)), ((), ())), preferred_element_type=F32)

    def tile_rows(j):
        return pl.ds(pl.multiple_of(j * tk, tk), tk)

    prev = jnp.maximum(i - 1, 0)
    k_both = jnp.concatenate([k_ref[tile_rows(i), :], k_ref[tile_rows(prev), :]], axis=0)
    v_prev = v_ref[tile_rows(prev), :]
    v_both = jnp.concatenate(
        [v_ref[tile_rows(i), :], jnp.where(i > 0, v_prev, jnp.zeros_like(v_prev))], axis=0)
    z = scores(k_both)
    rq = lax.broadcasted_iota(jnp.int32, (HEADS_PER_BLOCK * tq, tk), 0) % tq
    ck = lax.broadcasted_iota(jnp.int32, (HEADS_PER_BLOCK * tq, tk), 1)
    z_diag = jnp.where(ck < rq, z[:, :tk], MASKED_LOGIT)
    z_prev = z[:, tk:]
    within = _bdot(jnp.concatenate([_hi_lo(_softplus(z_diag)), _hi_lo(_softplus(z_prev))], axis=0),
                   tri2)
    within_diag, within_prev = within[:HEADS_PER_BLOCK * tq], within[HEADS_PER_BLOCK * tq:]
    carry = within_diag[:, 0:1]
    att = jnp.concatenate([jnp.exp(z_diag - within_diag),
                           jnp.exp(z_prev - (within_prev + carry))], axis=1)
    acc = _bdot(att.astype(BF16), v_both)
    carry = carry + within_prev[:, 0:1]

    def cond(state):
        j, low, _, _ = state
        return jnp.logical_and(j >= 0, low < EXP_UNDERFLOW)

    def body(state):
        j, _, acc, carry = state
        z = scores(k_ref[tile_rows(j), :])
        within = _bdot(_hi_lo(_softplus(z)), tri2)
        att = jnp.exp(z - (within + carry))
        acc = acc + _bdot(att.astype(BF16), v_ref[tile_rows(j), :])
        carry = carry + within[:, 0:1]
        return j - 1, jnp.min(carry), acc, carry

    _, _, acc, _ = lax.while_loop(cond, body, (i - 2, jnp.min(carry), acc, carry))
    out = acc[:tq]
    for h in range(1, HEADS_PER_BLOCK):
        out = jnp.where(lane // HEAD_DIM == h, acc[h * tq:(h + 1) * tq], out)
    o_ref[...] = out.astype(o_ref.dtype)


def _attn(q, k, v, batch, tq):
    t, width = q.shape
    seq = t // batch
    nq = seq // tq
    tk = tq
    later = (jnp.arange(2 * tk)[:, None] % tk) >= jnp.arange(tk)[None, :]
    q_spec = pl.BlockSpec((tq, LANES), lambda b, hp, i: (b * nq + i, hp))
    kv_spec = pl.BlockSpec((seq, LANES), lambda b, hp, i: (b, hp))
    return pl.pallas_call(
        _attn_kernel,
        grid=(batch, width // LANES, nq),
        in_specs=[q_spec, kv_spec, kv_spec, _resident((2 * tk, tk))],
        out_specs=q_spec,
        out_shape=jax.ShapeDtypeStruct((t, width), BF16),
        compiler_params=pltpu.CompilerParams(
            dimension_semantics=("arbitrary", "arbitrary", "arbitrary"),
            vmem_limit_bytes=VMEM_LIMIT),
        name="attn",
    )(q, k, v, later.astype(BF16))


def _merge_kernel(x_ref, yl_ref, ya_ref, gl_ref, ga_ref, wbl_ref, wba_ref, wo_ref, o_ref):
    merged = (_sigmoid(gl_ref[...]) * _bdot(yl_ref[...], wbl_ref[...])
              + _sigmoid(ga_ref[...]) * _bdot(ya_ref[...], wba_ref[...]))
    o_ref[...] = x_ref[...] + _bdot(merged.astype(BF16), wo_ref[...])


def _merge(x2, y_lru, y_att, g_lru, g_att, w_bl, w_ba, w_o, tm):
    t, d = x2.shape
    tile = pl.BlockSpec((tm, d), lambda i: (i, 0))
    wspec = _resident((d, d))
    return pl.pallas_call(
        _merge_kernel,
        grid=(t // tm,),
        in_specs=[tile] * 5 + [wspec] * 3,
        out_specs=tile,
        out_shape=jax.ShapeDtypeStruct((t, d), F32),
        compiler_params=pltpu.CompilerParams(
            dimension_semantics=("arbitrary",), vmem_limit_bytes=VMEM_LIMIT),
        name="merge",
    )(x2, y_lru, y_att, g_lru, g_att, w_bl, w_ba, w_o)


def _mlp_ple_kernel(x_ref, p_ref, g_mlp_ref, wu_ref, wd_ref, g_ple_ref, wpg_ref, wpl_ref,
                    g_fin_ref, o_ref):
    x = x_ref[...]
    h2 = _rmsnorm(x, g_mlp_ref[...]).astype(BF16)
    up = jnp.square(jnp.maximum(_bdot(h2, wu_ref[...]), 0.0))
    x = x + _bdot(up.astype(BF16), wd_ref[...])
    h3 = _rmsnorm(x, g_ple_ref[...]).astype(BF16)
    gate = _sigmoid(_bdot(h3, wpg_ref[...]))
    x = x + gate * _bdot(p_ref[...].astype(BF16), wpl_ref[...])
    o_ref[...] = _rmsnorm(x, g_fin_ref[...])


def _mlp_ple(x1, p2, g_mlp, w_up, w_down, g_ple, w_pg, w_pl, g_fin, tm):
    t, d = x1.shape
    tile = pl.BlockSpec((tm, d), lambda i: (i, 0))
    ptile = pl.BlockSpec((tm, p2.shape[1]), lambda i: (i, 0))
    row = _resident((1, d))
    return pl.pallas_call(
        _mlp_ple_kernel,
        grid=(t // tm,),
        in_specs=[tile, ptile, row, _resident(w_up.shape), _resident(w_down.shape), row,
                  _resident(w_pg.shape), _resident(w_pl.shape), row],
        out_specs=tile,
        out_shape=jax.ShapeDtypeStruct((t, d), F32),
        compiler_params=pltpu.CompilerParams(
            dimension_semantics=("arbitrary",), vmem_limit_bytes=VMEM_LIMIT),
        name="mlp_ple",
    )(x1, p2, g_mlp, w_up, w_down, g_ple, w_pg, w_pl, g_fin)


def kernel(x, p, norm_mix_g, w_in, conv_w, conv_b, w_rgate, b_rgate, w_igate, b_igate, lru_lambda, w_br_lru, w_br_att, w_out, norm_mlp_g, w_mlp_up, w_mlp_down, norm_ple_g, w_ple_gate, w_ple, norm_final_g):
    batch, seq, d = x.shape
    depth = w_in.shape[0]
    t = batch * seq
    x2 = x.reshape(t, d)
    for l in range(depth):
        q, k, v, g_lru, g_att, y_lru = _in_lru(
            x2, norm_mix_g[l].reshape(1, d), w_in[l].astype(BF16), conv_w[l], conv_b[l],
            w_rgate[l], b_rgate[l], w_igate[l], b_igate[l], lru_lambda[l], batch, ts=512)
        y_att = _attn(q, k, v, batch, tq=256)
        x2 = _merge(x2, y_lru, y_att, g_lru, g_att, w_br_lru[l].astype(BF16),
                    w_br_att[l].astype(BF16), w_out[l].astype(BF16), tm=512)
        last = l == depth - 1
        assert last, "deeper stacks need a mlp_ple variant without the final norm"
        x2 = _mlp_ple(x2, p[l].reshape(t, -1), norm_mlp_g[l].reshape(1, d),
                      w_mlp_up[l].astype(BF16), w_mlp_down[l].astype(BF16),
                      norm_ple_g[l].reshape(1, d), w_ple_gate[l].astype(BF16),
                      w_ple[l].astype(BF16), norm_final_g.reshape(1, d), tm=256)
    return x2.reshape(batch, seq, d)
```

```python
import functools

import jax
import jax.numpy as jnp
from jax import lax
from jax.experimental import pallas as pl
from jax.experimental.pallas import tpu as pltpu

F32 = jnp.float32
BF16 = jnp.bfloat16

NORM_EPS = 1e-6
LRU_C = 8.0
CONV_W = 4
HEAD_DIM = 64
LANES = 128
SUBLANES = 8
MXU_DIM = 256
HEADS_PER_BLOCK = LANES // HEAD_DIM

LOG2E = 1.4426950408889634
EXP_UNDERFLOW = 105.0
MASKED_LOGIT = -1e30

VMEM_LIMIT = 56 * 1024 * 1024


def _resident(shape):
    nd = len(shape)
    return pl.BlockSpec(shape, lambda *_: (0,) * nd, pipeline_mode=pl.Buffered(1))


def _rmsnorm(x, g):
    return x * lax.rsqrt(jnp.mean(x * x, axis=-1, keepdims=True) + NORM_EPS) * g


def _sigmoid(x):
    return 0.5 * jnp.tanh(0.5 * x) + 0.5


def _bdot(a, b):
    return jnp.dot(a, b, preferred_element_type=F32)


def _shift_rows(cur, prev_tail, back, row):
    rolled = pltpu.roll(cur, back, axis=0)
    first = jnp.where(row < back, pltpu.roll(prev_tail, back, axis=0), rolled[0:SUBLANES])
    return jnp.concatenate([first, rolled[SUBLANES:]], axis=0)


def _in_lru_kernel(x_ref, g_ref, w_ref, cw_ref, cb_ref, wr_ref, br_ref, wi_ref, bi_ref, lam_ref,
                   q_ref, k_ref, v_ref, gl_ref, ga_ref, y_ref,
                   tail, a_scr, u_scr, h_scr, hprev):
    ts, d = x_ref.shape

    @pl.when(pl.program_id(1) == 0)
    def _():
        tail[...] = jnp.zeros(tail.shape, F32)
        hprev[...] = jnp.zeros(hprev.shape, F32)

    h = _rmsnorm(x_ref[...], g_ref[...]).astype(BF16)

    def proj(n):
        return _bdot(h, w_ref[:, n * d:(n + 1) * d])

    outs = {2: q_ref, 3: k_ref, 4: v_ref, 5: gl_ref, 6: ga_ref}

    def emit(n):
        outs[n][...] = proj(n).astype(outs[n].dtype)

    ux = proj(0)
    ug = proj(1)
    row = lax.broadcasted_iota(jnp.int32, (SUBLANES, d), 0)
    prev_tail = tail[...]
    c = cb_ref[...]
    for k in range(CONV_W):
        back = CONV_W - 1 - k
        xs = ux if back == 0 else _shift_rows(ux, prev_tail, back, row)
        c = c + cw_ref[k:k + 1, :] * xs
    tail[...] = ux[ts - SUBLANES:, :]
    emit(2)

    cb = c.astype(BF16)
    gr, gi = [], []
    for j in range(d // MXU_DIM):
        sl = slice(j * MXU_DIM, (j + 1) * MXU_DIM)
        gr.append(_bdot(cb[:, sl], wr_ref[j]))
        gi.append(_bdot(cb[:, sl], wi_ref[j]))
    r = _sigmoid(jnp.concatenate(gr, axis=1) + br_ref[...])
    ig = _sigmoid(jnp.concatenate(gi, axis=1) + bi_ref[...])
    emit(3)

    neg_lam = -lam_ref[...]
    softplus = jnp.maximum(neg_lam, 0.0) + jnp.log1p(jnp.exp(-jnp.abs(neg_lam)))
    log_a = (-LRU_C) * r * softplus
    a = jnp.exp(log_a)
    one_m_a2 = jnp.tanh(-log_a) * (a * a + 1.0)
    mult = jnp.where(one_m_a2 > 0.0, one_m_a2 * lax.rsqrt(one_m_a2), 0.0)
    a_scr[...] = a
    u_scr[...] = mult * (ig * c)
    emit(4)

    h_in = hprev[...]
    for g0 in range(0, ts, SUBLANES):
        if g0 == ts // 2:
            emit(5)
        a8 = a_scr[g0:g0 + SUBLANES, :]
        u8 = u_scr[g0:g0 + SUBLANES, :]
        for dist in (1, 2, 4):
            keep = row >= dist
            u_sh = pltpu.roll(u8, dist, axis=0)
            a_sh = pltpu.roll(a8, dist, axis=0)
            u8 = jnp.where(keep, a8 * u_sh + u8, u8)
            a8 = jnp.where(keep, a8 * a_sh, a8)
        h8 = a8 * h_in + u8
        h_scr[g0:g0 + SUBLANES, :] = h8
        h_in = h8[SUBLANES - 1:SUBLANES, :]
    hprev[...] = h_in
    emit(6)

    y_ref[...] = (h_scr[...] * jax.nn.gelu(ug, approximate=True)).astype(y_ref.dtype)


def _block_diag(wb, group):
    n, c, _ = wb.shape
    wg = wb.reshape(n // group, group, c, c)
    eye = jnp.eye(group, dtype=wb.dtype)
    dense = wg[:, :, :, None, :] * eye[None, :, None, :, None]
    return dense.reshape(n // group, group * c, group * c)


def _in_lru(x2, g, w_bf, conv_w, conv_b, w_r, b_r, w_i, b_i, lam, batch, ts):
    t, d = x2.shape
    nt = t // batch // ts
    group = MXU_DIM // w_r.shape[1]
    wr = _block_diag(w_r, group).astype(BF16)
    wi = _block_diag(w_i, group).astype(BF16)
    tile = pl.BlockSpec((ts, d), lambda b, i: (b * nt + i, 0))
    row = _resident((1, d))
    dtypes = [BF16, BF16, BF16, F32, F32, BF16]
    return pl.pallas_call(
        _in_lru_kernel,
        grid=(batch, nt),
        in_specs=[tile, row, _resident(w_bf.shape), _resident(conv_w.shape), row,
                  _resident(wr.shape), row, _resident(wi.shape), row, row],
        out_specs=[tile] * len(dtypes),
        out_shape=[jax.ShapeDtypeStruct((t, d), dt) for dt in dtypes],
        scratch_shapes=[pltpu.VMEM((SUBLANES, d), F32), pltpu.VMEM((ts, d), F32),
                        pltpu.VMEM((ts, d), F32), pltpu.VMEM((ts, d), F32),
                        pltpu.VMEM((1, d), F32)],
        compiler_params=pltpu.CompilerParams(
            dimension_semantics=("arbitrary", "arbitrary"), vmem_limit_bytes=VMEM_LIMIT),
        name="in_lru",
    )(x2, g, w_bf, conv_w, conv_b.reshape(1, d), wr, b_r.reshape(1, d), wi, b_i.reshape(1, d),
      lam.reshape(1, d))


def _softplus(z):
    return jnp.maximum(z, 0.0) + jnp.log(1.0 + jnp.exp2(jnp.abs(z) * (-LOG2E)))


def _hi_lo(pos):
    hi = pos.astype(BF16)
    lo = (pos - hi.astype(F32)).astype(BF16)
    return jnp.concatenate([hi, lo], axis=1)


def _attn_kernel(qp_ref, qn_ref, k_ref, v_ref, tri_ref, o_ref, z_scr, att_scr, carry_scr):
    tq = qp_ref.shape[0]
    tk = tri_ref.shape[1]
    i = pl.program_id(2)
    nq = pl.num_programs(2) - 1
    lane = lax.broadcasted_iota(jnp.int32, (tq, LANES), 1)
    tri2 = tri_ref[...]

    def stack_heads(q):
        q = q * (HEAD_DIM ** -0.5)
        zero = jnp.zeros_like(q)
        return jnp.concatenate(
            [jnp.where(lane // HEAD_DIM == h, q, zero) for h in range(HEADS_PER_BLOCK)], axis=0)

    def tile_rows(j):
        return pl.ds(pl.multiple_of(j * tk, tk), tk)

    def scores(q2, k_tiles):
        return lax.dot_general(q2, k_tiles, (((1,), (1,)), ((), ())), preferred_element_type=F32)

    def first_scores(q2, j):
        prev = jnp.maximum(j - 1, 0)
        return scores(q2, jnp.concatenate([k_ref[tile_rows(j), :], k_ref[tile_rows(prev), :]],
                                          axis=0))

    @pl.when(i == 0)
    def _():
        z_scr[...] = first_scores(stack_heads(qp_ref[...]), 0)
        att_scr[...] = jnp.zeros(att_scr.shape, BF16)
        carry_scr[...] = jnp.full(carry_scr.shape, 2.0 * EXP_UNDERFLOW, F32)

    done = jnp.maximum(i - 1, 0)
    carry_done = carry_scr[...]
    v_prev = v_ref[tile_rows(jnp.maximum(done - 1, 0)), :]
    v_both = jnp.concatenate(
        [v_ref[tile_rows(done), :], jnp.where(done > 0, v_prev, jnp.zeros_like(v_prev))], axis=0)
    acc = _bdot(att_scr[...], v_both)

    z = z_scr[...]
    z_scr[...] = first_scores(stack_heads(qn_ref[...]), jnp.minimum(i + 1, nq - 1))
    rq = lax.broadcasted_iota(jnp.int32, (HEADS_PER_BLOCK * tq, tk), 0) % tq
    ck = lax.broadcasted_iota(jnp.int32, (HEADS_PER_BLOCK * tq, tk), 1)
    z_diag = jnp.where(ck < rq, z[:, :tk], MASKED_LOGIT)
    z_prev = z[:, tk:]
    within = _bdot(jnp.concatenate([_hi_lo(_softplus(z_diag)), _hi_lo(_softplus(z_prev))], axis=0),
                   tri2)
    within_diag, within_prev = within[:HEADS_PER_BLOCK * tq], within[HEADS_PER_BLOCK * tq:]
    carry = within_diag[:, 0:1]
    att_scr[...] = jnp.concatenate(
        [jnp.exp(z_diag - within_diag), jnp.exp(z_prev - (within_prev + carry))],
        axis=1).astype(BF16)
    carry_scr[...] = carry + within_prev[:, 0:1]

    def cond(state):
        j, low, _, _ = state
        return jnp.logical_and(j >= 0, low < EXP_UNDERFLOW)

    def body(state):
        j, _, acc, carry = state
        z = scores(stack_heads(qp_ref[...]), k_ref[tile_rows(j), :])
        within = _bdot(_hi_lo(_softplus(z)), tri2)
        att = jnp.exp(z - (within + carry))
        acc = acc + _bdot(att.astype(BF16), v_ref[tile_rows(j), :])
        carry = carry + within[:, 0:1]
        return j - 1, jnp.min(carry), acc, carry

    _, _, acc, _ = lax.while_loop(cond, body, (i - 3, jnp.min(carry_done), acc, carry_done))
    out = acc[:tq]
    for h in range(1, HEADS_PER_BLOCK):
        out = jnp.where(lane // HEAD_DIM == h, acc[h * tq:(h + 1) * tq], out)
    o_ref[...] = out.astype(o_ref.dtype)


def _attn(q, k, v, batch, tq):
    t, width = q.shape
    seq = t // batch
    nq = seq // tq
    tk = tq
    later = (jnp.arange(2 * tk)[:, None] % tk) >= jnp.arange(tk)[None, :]

    def q_tile(shift):
        return pl.BlockSpec(
            (tq, LANES), lambda b, hp, i: (b * nq + jnp.clip(i + shift, 0, nq - 1), hp))

    kv_spec = pl.BlockSpec((seq, LANES), lambda b, hp, i: (b, hp))
    rows = HEADS_PER_BLOCK * tq
    return pl.pallas_call(
        _attn_kernel,
        grid=(batch, width // LANES, nq + 1),
        in_specs=[q_tile(-1), q_tile(1), kv_spec, kv_spec, _resident((2 * tk, tk))],
        out_specs=q_tile(-1),
        out_shape=jax.ShapeDtypeStruct((t, width), BF16),
        scratch_shapes=[pltpu.VMEM((rows, 2 * tk), F32), pltpu.VMEM((rows, 2 * tk), BF16),
                        pltpu.VMEM((rows, 1), F32)],
        compiler_params=pltpu.CompilerParams(
            dimension_semantics=("arbitrary", "arbitrary", "arbitrary"),
            vmem_limit_bytes=VMEM_LIMIT),
        name="attn",
    )(q, q, k, v, later.astype(BF16))


def _merge_kernel(x_ref, yl_ref, ya_ref, gl_ref, ga_ref, wbl_ref, wba_ref, wo_ref, o_ref):
    merged = (_sigmoid(gl_ref[...]) * _bdot(yl_ref[...], wbl_ref[...])
              + _sigmoid(ga_ref[...]) * _bdot(ya_ref[...], wba_ref[...]))
    o_ref[...] = x_ref[...] + _bdot(merged.astype(BF16), wo_ref[...])


def _merge(x2, y_lru, y_att, g_lru, g_att, w_bl, w_ba, w_o, tm):
    t, d = x2.shape
    tile = pl.BlockSpec((tm, d), lambda i: (i, 0))
    wspec = _resident((d, d))
    return pl.pallas_call(
        _merge_kernel,
        grid=(t // tm,),
        in_specs=[tile] * 5 + [wspec] * 3,
        out_specs=tile,
        out_shape=jax.ShapeDtypeStruct((t, d), F32),
        compiler_params=pltpu.CompilerParams(
            dimension_semantics=("arbitrary",), vmem_limit_bytes=VMEM_LIMIT),
        name="merge",
    )(x2, y_lru, y_att, g_lru, g_att, w_bl, w_ba, w_o)


def _mlp_ple_kernel(x_ref, p_ref, g_mlp_ref, wu_ref, wd_ref, g_ple_ref, wpg_ref, wpl_ref,
                    g_fin_ref, o_ref):
    x = x_ref[...]
    h2 = _rmsnorm(x, g_mlp_ref[...]).astype(BF16)
    up = jnp.square(jnp.maximum(_bdot(h2, wu_ref[...]), 0.0))
    x = x + _bdot(up.astype(BF16), wd_ref[...])
    h3 = _rmsnorm(x, g_ple_ref[...]).astype(BF16)
    gate = _sigmoid(_bdot(h3, wpg_ref[...]))
    x = x + gate * _bdot(p_ref[...].astype(BF16), wpl_ref[...])
    o_ref[...] = _rmsnorm(x, g_fin_ref[...])


def _mlp_ple(x1, p2, g_mlp, w_up, w_down, g_ple, w_pg, w_pl, g_fin, tm):
    t, d = x1.shape
    tile = pl.BlockSpec((tm, d), lambda i: (i, 0))
    ptile = pl.BlockSpec((tm, p2.shape[1]), lambda i: (i, 0))
    row = _resident((1, d))
    return pl.pallas_call(
        _mlp_ple_kernel,
        grid=(t // tm,),
        in_specs=[tile, ptile, row, _resident(w_up.shape), _resident(w_down.shape), row,
                  _resident(w_pg.shape), _resident(w_pl.shape), row],
        out_specs=tile,
        out_shape=jax.ShapeDtypeStruct((t, d), F32),
        compiler_params=pltpu.CompilerParams(
            dimension_semantics=("arbitrary",), vmem_limit_bytes=VMEM_LIMIT),
        name="mlp_ple",
    )(x1, p2, g_mlp, w_up, w_down, g_ple, w_pg, w_pl, g_fin)


def kernel(x, p, norm_mix_g, w_in, conv_w, conv_b, w_rgate, b_rgate, w_igate, b_igate, lru_lambda, w_br_lru, w_br_att, w_out, norm_mlp_g, w_mlp_up, w_mlp_down, norm_ple_g, w_ple_gate, w_ple, norm_final_g):
    batch, seq, d = x.shape
    depth = w_in.shape[0]
    t = batch * seq
    x2 = x.reshape(t, d)
    for l in range(depth):
        q, k, v, g_lru, g_att, y_lru = _in_lru(
            x2, norm_mix_g[l].reshape(1, d), w_in[l].astype(BF16), conv_w[l], conv_b[l],
            w_rgate[l], b_rgate[l], w_igate[l], b_igate[l], lru_lambda[l], batch, ts=512)
        y_att = _attn(q, k, v, batch, tq=256)
        x2 = _merge(x2, y_lru, y_att, g_lru, g_att, w_br_lru[l].astype(BF16),
                    w_br_att[l].astype(BF16), w_out[l].astype(BF16), tm=512)
        last = l == depth - 1
        assert last, "deeper stacks need a mlp_ple variant without the final norm"
        x2 = _mlp_ple(x2, p[l].reshape(t, -1), norm_mlp_g[l].reshape(1, d),
                      w_mlp_up[l].astype(BF16), w_mlp_down[l].astype(BF16),
                      norm_ple_g[l].reshape(1, d), w_ple_gate[l].astype(BF16),
                      w_ple[l].astype(BF16), norm_final_g.reshape(1, d), tm=256)
    return x2.reshape(batch, seq, d)
```

```python
import jax
import jax.numpy as jnp
from jax import lax
from jax.experimental import pallas as pl
from jax.experimental.pallas import tpu as pltpu

F32 = jnp.float32
BF16 = jnp.bfloat16

NORM_EPS = 1e-6
LRU_C = 8.0
CONV_W = 4
HEAD_DIM = 64
LANES = 128
SUBLANES = 8
MXU_DIM = 256
HEADS_PER_BLOCK = LANES // HEAD_DIM

LOG2E = 1.4426950408889634
EXP_UNDERFLOW = 105.0
MASKED_LOGIT = -1e30

VMEM_LIMIT = 56 * 1024 * 1024


def _resident(shape):
    nd = len(shape)
    return pl.BlockSpec(shape, lambda *_: (0,) * nd, pipeline_mode=pl.Buffered(1))


def _rmsnorm(x, g):
    return x * lax.rsqrt(jnp.mean(x * x, axis=-1, keepdims=True) + NORM_EPS) * g


def _sigmoid(x):
    return 0.5 * jnp.tanh(0.5 * x) + 0.5


def _bdot(a, b):
    return jnp.dot(a, b, preferred_element_type=F32)


def _shift_rows(cur, prev_tail, back, row):
    rolled = pltpu.roll(cur, back, axis=0)
    first = jnp.where(row < back, pltpu.roll(prev_tail, back, axis=0), rolled[0:SUBLANES])
    return jnp.concatenate([first, rolled[SUBLANES:]], axis=0)


def _in_lru_kernel(x_ref, g_ref, w_ref, cw_ref, cb_ref, wr_ref, br_ref, wi_ref, bi_ref, lam_ref,
                   q_ref, k_ref, v_ref, gl_ref, ga_ref, y_ref,
                   tail, a_scr, u_scr, h_scr, hprev):
    ts, d = x_ref.shape

    @pl.when(pl.program_id(1) == 0)
    def _():
        tail[...] = jnp.zeros(tail.shape, F32)
        hprev[...] = jnp.zeros(hprev.shape, F32)

    h = _rmsnorm(x_ref[...], g_ref[...]).astype(BF16)

    def proj(n):
        return _bdot(h, w_ref[:, n * d:(n + 1) * d])

    outs = {2: q_ref, 3: k_ref, 4: v_ref, 5: gl_ref, 6: ga_ref}

    def emit(n):
        outs[n][...] = proj(n).astype(outs[n].dtype)

    ux = proj(0)
    ug = proj(1)
    row = lax.broadcasted_iota(jnp.int32, (SUBLANES, d), 0)
    prev_tail = tail[...]
    c = cb_ref[...]
    for k in range(CONV_W):
        back = CONV_W - 1 - k
        xs = ux if back == 0 else _shift_rows(ux, prev_tail, back, row)
        c = c + cw_ref[k:k + 1, :] * xs
    tail[...] = ux[ts - SUBLANES:, :]
    emit(2)

    cb = c.astype(BF16)
    gr, gi = [], []
    for j in range(d // MXU_DIM):
        sl = slice(j * MXU_DIM, (j + 1) * MXU_DIM)
        gr.append(_bdot(cb[:, sl], wr_ref[j]))
        gi.append(_bdot(cb[:, sl], wi_ref[j]))
    r = _sigmoid(jnp.concatenate(gr, axis=1) + br_ref[...])
    ig = _sigmoid(jnp.concatenate(gi, axis=1) + bi_ref[...])
    emit(3)

    neg_lam = -lam_ref[...]
    softplus = jnp.maximum(neg_lam, 0.0) + jnp.log1p(jnp.exp(-jnp.abs(neg_lam)))
    log_a = (-LRU_C) * r * softplus
    a = jnp.exp(log_a)
    one_m_a2 = jnp.tanh(-log_a) * (a * a + 1.0)
    mult = jnp.where(one_m_a2 > 0.0, one_m_a2 * lax.rsqrt(one_m_a2), 0.0)
    a_scr[...] = a
    u_scr[...] = mult * (ig * c)
    emit(4)

    h_in = hprev[...]
    for g0 in range(0, ts, SUBLANES):
        if g0 == ts // 2:
            emit(5)
        a8 = a_scr[g0:g0 + SUBLANES, :]
        u8 = u_scr[g0:g0 + SUBLANES, :]
        for dist in (1, 2, 4):
            keep = row >= dist
            u_sh = pltpu.roll(u8, dist, axis=0)
            a_sh = pltpu.roll(a8, dist, axis=0)
            u8 = jnp.where(keep, a8 * u_sh + u8, u8)
            a8 = jnp.where(keep, a8 * a_sh, a8)
        h8 = a8 * h_in + u8
        h_scr[g0:g0 + SUBLANES, :] = h8
        h_in = h8[SUBLANES - 1:SUBLANES, :]
    hprev[...] = h_in
    emit(6)

    y_ref[...] = (h_scr[...] * jax.nn.gelu(ug, approximate=True)).astype(y_ref.dtype)


def _block_diag(wb, group):
    n, c, _ = wb.shape
    wg = wb.reshape(n // group, group, c, c)
    eye = jnp.eye(group, dtype=wb.dtype)
    dense = wg[:, :, :, None, :] * eye[None, :, None, :, None]
    return dense.reshape(n // group, group * c, group * c)


def _in_lru(x2, g, w_bf, conv_w, conv_b, w_r, b_r, w_i, b_i, lam, batch, ts):
    t, d = x2.shape
    nt = t // batch // ts
    group = MXU_DIM // w_r.shape[1]
    wr = _block_diag(w_r, group).astype(BF16)
    wi = _block_diag(w_i, group).astype(BF16)
    tile = pl.BlockSpec((ts, d), lambda b, i: (b * nt + i, 0))
    row = _resident((1, d))
    dtypes = [BF16, BF16, BF16, F32, F32, BF16]
    return pl.pallas_call(
        _in_lru_kernel,
        grid=(batch, nt),
        in_specs=[tile, row, _resident(w_bf.shape), _resident(conv_w.shape), row,
                  _resident(wr.shape), row, _resident(wi.shape), row, row],
        out_specs=[tile] * len(dtypes),
        out_shape=[jax.ShapeDtypeStruct((t, d), dt) for dt in dtypes],
        scratch_shapes=[pltpu.VMEM((SUBLANES, d), F32), pltpu.VMEM((ts, d), F32),
                        pltpu.VMEM((ts, d), F32), pltpu.VMEM((ts, d), F32),
                        pltpu.VMEM((1, d), F32)],
        compiler_params=pltpu.CompilerParams(
            dimension_semantics=("arbitrary", "arbitrary"), vmem_limit_bytes=VMEM_LIMIT),
        name="in_lru",
    )(x2, g, w_bf, conv_w, conv_b.reshape(1, d), wr, b_r.reshape(1, d), wi, b_i.reshape(1, d),
      lam.reshape(1, d))


def _softplus(z):
    return jnp.maximum(z, 0.0) + jnp.log(1.0 + jnp.exp2(jnp.abs(z) * (-LOG2E)))


def _attn_kernel(qp_ref, qn_ref, k_ref, v_ref, tri_ref, o_ref, z_scr, att_scr, carry_scr):
    tq = qp_ref.shape[0]
    tk = tri_ref.shape[1]
    i = pl.program_id(2)
    nq = pl.num_programs(2) - 1
    lane = lax.broadcasted_iota(jnp.int32, (tq, LANES), 1)
    tri = tri_ref[...]

    def stack_heads(q):
        q = q * (HEAD_DIM ** -0.5)
        zero = jnp.zeros_like(q)
        return jnp.concatenate(
            [jnp.where(lane // HEAD_DIM == h, q, zero) for h in range(HEADS_PER_BLOCK)], axis=0)

    def tile_rows(j):
        return pl.ds(pl.multiple_of(j * tk, tk), tk)

    def scores(q2, k_tiles):
        return lax.dot_general(q2, k_tiles, (((1,), (1,)), ((), ())), preferred_element_type=F32)

    def first_scores(q2, j):
        prev = jnp.maximum(j - 1, 0)
        return scores(q2, jnp.concatenate([k_ref[tile_rows(j), :], k_ref[tile_rows(prev), :]],
                                          axis=0))

    def later_sums(z):
        return _bdot(_softplus(z).astype(BF16), tri)

    @pl.when(i == 0)
    def _():
        z_scr[...] = first_scores(stack_heads(qp_ref[...]), 0)
        att_scr[...] = jnp.zeros(att_scr.shape, BF16)
        carry_scr[...] = jnp.full(carry_scr.shape, 2.0 * EXP_UNDERFLOW, F32)

    done = jnp.maximum(i - 1, 0)
    carry_done = carry_scr[...]
    v_prev = v_ref[tile_rows(jnp.maximum(done - 1, 0)), :]
    v_both = jnp.concatenate(
        [v_ref[tile_rows(done), :], jnp.where(done > 0, v_prev, jnp.zeros_like(v_prev))], axis=0)
    acc = _bdot(att_scr[...], v_both)

    z = z_scr[...]
    z_scr[...] = first_scores(stack_heads(qn_ref[...]), jnp.minimum(i + 1, nq - 1))
    rq = lax.broadcasted_iota(jnp.int32, (HEADS_PER_BLOCK * tq, tk), 0) % tq
    ck = lax.broadcasted_iota(jnp.int32, (HEADS_PER_BLOCK * tq, tk), 1)
    z_diag = jnp.where(ck < rq, z[:, :tk], MASKED_LOGIT)
    z_prev = z[:, tk:]
    within = later_sums(jnp.concatenate([z_diag, z_prev], axis=0))
    within_diag, within_prev = within[:HEADS_PER_BLOCK * tq], within[HEADS_PER_BLOCK * tq:]
    carry = within_diag[:, 0:1]
    att_scr[...] = jnp.concatenate(
        [jnp.exp(z_diag - within_diag), jnp.exp(z_prev - (within_prev + carry))],
        axis=1).astype(BF16)
    carry_scr[...] = carry + within_prev[:, 0:1]

    def cond(state):
        j, low, _, _ = state
        return jnp.logical_and(j >= 0, low < EXP_UNDERFLOW)

    def body(state):
        j, _, acc, carry = state
        z = scores(stack_heads(qp_ref[...]), k_ref[tile_rows(j), :])
        within = later_sums(z)
        att = jnp.exp(z - (within + carry))
        acc = acc + _bdot(att.astype(BF16), v_ref[tile_rows(j), :])
        carry = carry + within[:, 0:1]
        return j - 1, jnp.min(carry), acc, carry

    _, _, acc, _ = lax.while_loop(cond, body, (i - 3, jnp.min(carry_done), acc, carry_done))
    out = acc[:tq]
    for h in range(1, HEADS_PER_BLOCK):
        out = jnp.where(lane // HEAD_DIM == h, acc[h * tq:(h + 1) * tq], out)
    o_ref[...] = out.astype(o_ref.dtype)


def _attn(q, k, v, batch, tq):
    t, width = q.shape
    seq = t // batch
    nq = seq // tq
    tk = tq
    later = jnp.arange(tk)[:, None] >= jnp.arange(tk)[None, :]

    def q_tile(shift):
        return pl.BlockSpec(
            (tq, LANES), lambda b, hp, i: (b * nq + jnp.clip(i + shift, 0, nq - 1), hp))

    kv_spec = pl.BlockSpec((seq, LANES), lambda b, hp, i: (b, hp))
    rows = HEADS_PER_BLOCK * tq
    return pl.pallas_call(
        _attn_kernel,
        grid=(batch, width // LANES, nq + 1),
        in_specs=[q_tile(-1), q_tile(1), kv_spec, kv_spec, _resident((tk, tk))],
        out_specs=q_tile(-1),
        out_shape=jax.ShapeDtypeStruct((t, width), BF16),
        scratch_shapes=[pltpu.VMEM((rows, 2 * tk), F32), pltpu.VMEM((rows, 2 * tk), BF16),
                        pltpu.VMEM((rows, 1), F32)],
        compiler_params=pltpu.CompilerParams(
            dimension_semantics=("arbitrary", "arbitrary", "arbitrary"),
            vmem_limit_bytes=VMEM_LIMIT),
        name="attn",
    )(q, q, k, v, later.astype(BF16))


def _merge_kernel(x_ref, yl_ref, ya_ref, gl_ref, ga_ref, wbl_ref, wba_ref, wo_ref, o_ref):
    merged = (_sigmoid(gl_ref[...]) * _bdot(yl_ref[...], wbl_ref[...])
              + _sigmoid(ga_ref[...]) * _bdot(ya_ref[...], wba_ref[...]))
    o_ref[...] = x_ref[...] + _bdot(merged.astype(BF16), wo_ref[...])


def _merge(x2, y_lru, y_att, g_lru, g_att, w_bl, w_ba, w_o, tm):
    t, d = x2.shape
    tile = pl.BlockSpec((tm, d), lambda i: (i, 0))
    wspec = _resident((d, d))
    return pl.pallas_call(
        _merge_kernel,
        grid=(t // tm,),
        in_specs=[tile] * 5 + [wspec] * 3,
        out_specs=tile,
        out_shape=jax.ShapeDtypeStruct((t, d), F32),
        compiler_params=pltpu.CompilerParams(
            dimension_semantics=("arbitrary",), vmem_limit_bytes=VMEM_LIMIT),
        name="merge",
    )(x2, y_lru, y_att, g_lru, g_att, w_bl, w_ba, w_o)


def _mlp_ple_kernel(x_ref, p_ref, g_mlp_ref, wu_ref, wd_ref, g_ple_ref, wpg_ref, wpl_ref,
                    g_fin_ref, o_ref):
    x = x_ref[...]
    h2 = _rmsnorm(x, g_mlp_ref[...]).astype(BF16)
    up = jnp.square(jnp.maximum(_bdot(h2, wu_ref[...]), 0.0))
    x = x + _bdot(up.astype(BF16), wd_ref[...])
    h3 = _rmsnorm(x, g_ple_ref[...]).astype(BF16)
    gate = _sigmoid(_bdot(h3, wpg_ref[...]))
    x = x + gate * _bdot(p_ref[...].astype(BF16), wpl_ref[...])
    o_ref[...] = _rmsnorm(x, g_fin_ref[...])


def _mlp_ple(x1, p2, g_mlp, w_up, w_down, g_ple, w_pg, w_pl, g_fin, tm):
    t, d = x1.shape
    tile = pl.BlockSpec((tm, d), lambda i: (i, 0))
    ptile = pl.BlockSpec((tm, p2.shape[1]), lambda i: (i, 0))
    row = _resident((1, d))
    return pl.pallas_call(
        _mlp_ple_kernel,
        grid=(t // tm,),
        in_specs=[tile, ptile, row, _resident(w_up.shape), _resident(w_down.shape), row,
                  _resident(w_pg.shape), _resident(w_pl.shape), row],
        out_specs=tile,
        out_shape=jax.ShapeDtypeStruct((t, d), F32),
        compiler_params=pltpu.CompilerParams(
            dimension_semantics=("arbitrary",), vmem_limit_bytes=VMEM_LIMIT),
        name="mlp_ple",
    )(x1, p2, g_mlp, w_up, w_down, g_ple, w_pg, w_pl, g_fin)


def kernel(x, p, norm_mix_g, w_in, conv_w, conv_b, w_rgate, b_rgate, w_igate, b_igate, lru_lambda, w_br_lru, w_br_att, w_out, norm_mlp_g, w_mlp_up, w_mlp_down, norm_ple_g, w_ple_gate, w_ple, norm_final_g):
    batch, seq, d = x.shape
    depth = w_in.shape[0]
    t = batch * seq
    x2 = x.reshape(t, d)
    for l in range(depth):
        q, k, v, g_lru, g_att, y_lru = _in_lru(
            x2, norm_mix_g[l].reshape(1, d), w_in[l].astype(BF16), conv_w[l], conv_b[l],
            w_rgate[l], b_rgate[l], w_igate[l], b_igate[l], lru_lambda[l], batch, ts=512)
        y_att = _attn(q, k, v, batch, tq=256)
        x2 = _merge(x2, y_lru, y_att, g_lru, g_att, w_br_lru[l].astype(BF16),
                    w_br_att[l].astype(BF16), w_out[l].astype(BF16), tm=512)
        last = l == depth - 1
        assert last, "deeper stacks need a mlp_ple variant without the final norm"
        x2 = _mlp_ple(x2, p[l].reshape(t, -1), norm_mlp_g[l].reshape(1, d),
                      w_mlp_up[l].astype(BF16), w_mlp_down[l].astype(BF16),
                      norm_ple_g[l].reshape(1, d), w_ple_gate[l].astype(BF16),
                      w_ple[l].astype(BF16), norm_final_g.reshape(1, d), tm=256)
    return x2.reshape(batch, seq, d)
```

```python
import jax
import jax.numpy as jnp
from jax import lax
from jax.experimental import pallas as pl
from jax.experimental.pallas import tpu as pltpu

F32 = jnp.float32
BF16 = jnp.bfloat16

NORM_EPS = 1e-6
LRU_C = 8.0
CONV_W = 4
HEAD_DIM = 64
SUBLANES = 8
MXU_DIM = 256
ATTN_LANES = 128
HEADS_PER_BLOCK = ATTN_LANES // HEAD_DIM

LOG2E = 1.4426950408889634
EXP_UNDERFLOW = 105.0
MASKED_LOGIT = -1e30

VMEM_LIMIT = 56 * 1024 * 1024


def _resident(shape):
    nd = len(shape)
    return pl.BlockSpec(shape, lambda *_: (0,) * nd, pipeline_mode=pl.Buffered(1))


def _rmsnorm(x, g):
    return x * lax.rsqrt(jnp.mean(x * x, axis=-1, keepdims=True) + NORM_EPS) * g


def _sigmoid(x):
    return 0.5 * jnp.tanh(0.5 * x) + 0.5


def _bdot(a, b):
    return jnp.dot(a, b, preferred_element_type=F32)


def _shift_rows(cur, prev_tail, back, row):
    rolled = pltpu.roll(cur, back, axis=0)
    first = jnp.where(row < back, pltpu.roll(prev_tail, back, axis=0), rolled[0:SUBLANES])
    return jnp.concatenate([first, rolled[SUBLANES:]], axis=0)


def _in_lru_kernel(x_ref, g_ref, w_ref, cw_ref, cb_ref, wr_ref, br_ref, wi_ref, bi_ref, lam_ref,
                   q_ref, k_ref, v_ref, gl_ref, ga_ref, y_ref,
                   tail, a_scr, u_scr, h_scr, hprev):
    ts, d = x_ref.shape

    @pl.when(pl.program_id(1) == 0)
    def _():
        tail[...] = jnp.zeros(tail.shape, F32)
        hprev[...] = jnp.zeros(hprev.shape, F32)

    h = _rmsnorm(x_ref[...], g_ref[...]).astype(BF16)

    def proj(n):
        return _bdot(h, w_ref[:, n * d:(n + 1) * d])

    outs = {2: q_ref, 3: k_ref, 4: v_ref, 5: gl_ref, 6: ga_ref}

    def emit(n):
        outs[n][...] = proj(n).astype(outs[n].dtype)

    ux = proj(0)
    ug = proj(1)
    row = lax.broadcasted_iota(jnp.int32, (SUBLANES, d), 0)
    prev_tail = tail[...]
    c = cb_ref[...]
    for k in range(CONV_W):
        back = CONV_W - 1 - k
        xs = ux if back == 0 else _shift_rows(ux, prev_tail, back, row)
        c = c + cw_ref[k:k + 1, :] * xs
    tail[...] = ux[ts - SUBLANES:, :]
    emit(2)

    cb = c.astype(BF16)
    gr, gi = [], []
    for j in range(d // MXU_DIM):
        sl = slice(j * MXU_DIM, (j + 1) * MXU_DIM)
        gr.append(_bdot(cb[:, sl], wr_ref[j]))
        gi.append(_bdot(cb[:, sl], wi_ref[j]))
    r = _sigmoid(jnp.concatenate(gr, axis=1) + br_ref[...])
    ig = _sigmoid(jnp.concatenate(gi, axis=1) + bi_ref[...])
    emit(3)

    neg_lam = -lam_ref[...]
    softplus = jnp.maximum(neg_lam, 0.0) + jnp.log1p(jnp.exp(-jnp.abs(neg_lam)))
    log_a = (-LRU_C) * r * softplus
    a = jnp.exp(log_a)
    one_m_a2 = jnp.tanh(-log_a) * (a * a + 1.0)
    mult = jnp.where(one_m_a2 > 0.0, one_m_a2 * lax.rsqrt(one_m_a2), 0.0)
    a_scr[...] = a
    u_scr[...] = mult * (ig * c)
    emit(4)

    h_in = hprev[...]
    for g0 in range(0, ts, SUBLANES):
        if g0 == ts // 2:
            emit(5)
        a8 = a_scr[g0:g0 + SUBLANES, :]
        u8 = u_scr[g0:g0 + SUBLANES, :]
        for dist in (1, 2, 4):
            keep = row >= dist
            u_sh = pltpu.roll(u8, dist, axis=0)
            a_sh = pltpu.roll(a8, dist, axis=0)
            u8 = jnp.where(keep, a8 * u_sh + u8, u8)
            a8 = jnp.where(keep, a8 * a_sh, a8)
        h8 = a8 * h_in + u8
        h_scr[g0:g0 + SUBLANES, :] = h8
        h_in = h8[SUBLANES - 1:SUBLANES, :]
    hprev[...] = h_in
    emit(6)

    y_ref[...] = (h_scr[...] * jax.nn.gelu(ug, approximate=True)).astype(y_ref.dtype)


def _block_diag(wb, group):
    n, c, _ = wb.shape
    wg = wb.reshape(n // group, group, c, c)
    eye = jnp.eye(group, dtype=wb.dtype)
    dense = wg[:, :, :, None, :] * eye[None, :, None, :, None]
    return dense.reshape(n // group, group * c, group * c)


def _in_lru(x2, g, w_bf, conv_w, conv_b, w_r, b_r, w_i, b_i, lam, batch, ts):
    t, d = x2.shape
    nt = t // batch // ts
    group = MXU_DIM // w_r.shape[1]
    wr = _block_diag(w_r, group).astype(BF16)
    wi = _block_diag(w_i, group).astype(BF16)
    tile = pl.BlockSpec((ts, d), lambda b, i: (b * nt + i, 0))
    row = _resident((1, d))
    dtypes = [BF16, BF16, BF16, F32, F32, BF16]
    return pl.pallas_call(
        _in_lru_kernel,
        grid=(batch, nt),
        in_specs=[tile, row, _resident(w_bf.shape), _resident(conv_w.shape), row,
                  _resident(wr.shape), row, _resident(wi.shape), row, row],
        out_specs=[tile] * len(dtypes),
        out_shape=[jax.ShapeDtypeStruct((t, d), dt) for dt in dtypes],
        scratch_shapes=[pltpu.VMEM((SUBLANES, d), F32), pltpu.VMEM((ts, d), F32),
                        pltpu.VMEM((ts, d), F32), pltpu.VMEM((ts, d), F32),
                        pltpu.VMEM((1, d), F32)],
        compiler_params=pltpu.CompilerParams(
            dimension_semantics=("arbitrary", "arbitrary"), vmem_limit_bytes=VMEM_LIMIT),
        name="in_lru",
    )(x2, g, w_bf, conv_w, conv_b.reshape(1, d), wr, b_r.reshape(1, d), wi, b_i.reshape(1, d),
      lam.reshape(1, d))


def _softplus(z):
    return jnp.maximum(z, 0.0) + jnp.log(1.0 + jnp.exp2(jnp.abs(z) * (-LOG2E)))


def _attn_kernel(qp_ref, qn_ref, k_ref, v_ref, tri_ref, o_ref, z_scr, att_scr, carry_scr):
    tq = qp_ref.shape[0]
    tk = tri_ref.shape[1]
    i = pl.program_id(2)
    nq = pl.num_programs(2) - 1
    lane = lax.broadcasted_iota(jnp.int32, (tq, ATTN_LANES), 1)
    tri = tri_ref[...]

    def stack_heads(q):
        q = q * (HEAD_DIM ** -0.5)
        zero = jnp.zeros_like(q)
        return jnp.concatenate(
            [jnp.where(lane // HEAD_DIM == h, q, zero) for h in range(HEADS_PER_BLOCK)], axis=0)

    def tile_rows(j):
        return pl.ds(pl.multiple_of(j * tk, tk), tk)

    def scores(q2, k_tiles):
        return lax.dot_general(q2, k_tiles, (((1,), (1,)), ((), ())), preferred_element_type=F32)

    def first_scores(q2, j):
        prev = jnp.maximum(j - 1, 0)
        return scores(q2, jnp.concatenate([k_ref[tile_rows(j), :], k_ref[tile_rows(prev), :]],
                                          axis=0))

    def later_sums(z):
        return _bdot(_softplus(z).astype(BF16), tri)

    @pl.when(i == 0)
    def _():
        z_scr[...] = first_scores(stack_heads(qp_ref[...]), 0)
        att_scr[...] = jnp.zeros(att_scr.shape, BF16)
        carry_scr[...] = jnp.full(carry_scr.shape, 2.0 * EXP_UNDERFLOW, F32)

    done = jnp.maximum(i - 1, 0)
    carry_done = carry_scr[...]
    v_prev = v_ref[tile_rows(jnp.maximum(done - 1, 0)), :]
    v_both = jnp.concatenate(
        [v_ref[tile_rows(done), :], jnp.where(done > 0, v_prev, jnp.zeros_like(v_prev))], axis=0)
    acc = _bdot(att_scr[...], v_both)

    z = z_scr[...]
    z_scr[...] = first_scores(stack_heads(qn_ref[...]), jnp.minimum(i + 1, nq - 1))
    rq = lax.broadcasted_iota(jnp.int32, (HEADS_PER_BLOCK * tq, tk), 0) % tq
    ck = lax.broadcasted_iota(jnp.int32, (HEADS_PER_BLOCK * tq, tk), 1)
    z_diag = jnp.where(ck < rq, z[:, :tk], MASKED_LOGIT)
    z_prev = z[:, tk:]
    within = later_sums(jnp.concatenate([z_diag, z_prev], axis=0))
    within_diag, within_prev = within[:HEADS_PER_BLOCK * tq], within[HEADS_PER_BLOCK * tq:]
    carry = within_diag[:, 0:1]
    att_scr[...] = jnp.concatenate(
        [jnp.exp(z_diag - within_diag), jnp.exp(z_prev - (within_prev + carry))],
        axis=1).astype(BF16)
    carry_scr[...] = carry + within_prev[:, 0:1]

    def cond(state):
        j, low, _, _ = state
        return jnp.logical_and(j >= 0, low < EXP_UNDERFLOW)

    def body(state):
        j, _, acc, carry = state
        z = scores(stack_heads(qp_ref[...]), k_ref[tile_rows(j), :])
        within = later_sums(z)
        att = jnp.exp(z - (within + carry))
        acc = acc + _bdot(att.astype(BF16), v_ref[tile_rows(j), :])
        carry = carry + within[:, 0:1]
        return j - 1, jnp.min(carry), acc, carry

    _, _, acc, _ = lax.while_loop(cond, body, (i - 3, jnp.min(carry_done), acc, carry_done))
    out = acc[:tq]
    for h in range(1, HEADS_PER_BLOCK):
        out = jnp.where(lane // HEAD_DIM == h, acc[h * tq:(h + 1) * tq], out)
    o_ref[...] = out.astype(o_ref.dtype)


def _attn(q, k, v, batch, tq):
    t, width = q.shape
    seq = t // batch
    nq = seq // tq
    tk = tq
    later = jnp.arange(tk)[:, None] >= jnp.arange(tk)[None, :]

    def q_tile(shift):
        return pl.BlockSpec(
            (tq, ATTN_LANES), lambda b, hp, i: (b * nq + jnp.clip(i + shift, 0, nq - 1), hp))

    kv_spec = pl.BlockSpec((seq, ATTN_LANES), lambda b, hp, i: (b, hp))
    rows = HEADS_PER_BLOCK * tq
    return pl.pallas_call(
        _attn_kernel,
        grid=(batch, width // ATTN_LANES, nq + 1),
        in_specs=[q_tile(-1), q_tile(1), kv_spec, kv_spec, _resident((tk, tk))],
        out_specs=q_tile(-1),
        out_shape=jax.ShapeDtypeStruct((t, width), BF16),
        scratch_shapes=[pltpu.VMEM((rows, 2 * tk), F32), pltpu.VMEM((rows, 2 * tk), BF16),
                        pltpu.VMEM((rows, 1), F32)],
        compiler_params=pltpu.CompilerParams(
            dimension_semantics=("arbitrary", "arbitrary", "arbitrary"),
            vmem_limit_bytes=VMEM_LIMIT),
        name="attn",
    )(q, q, k, v, later.astype(BF16))


def _post_kernel(x_ref, yl_ref, ya_ref, gl_ref, ga_ref, p_ref, wbl_ref, wba_ref, wo_ref,
                 g_mlp_ref, wu_ref, wd_ref, g_ple_ref, wpg_ref, wpl_ref, g_fin_ref, o_ref):
    merged = (_sigmoid(gl_ref[...]) * _bdot(yl_ref[...], wbl_ref[...])
              + _sigmoid(ga_ref[...]) * _bdot(ya_ref[...], wba_ref[...]))
    x = x_ref[...] + _bdot(merged.astype(BF16), wo_ref[...])
    h2 = _rmsnorm(x, g_mlp_ref[...]).astype(BF16)
    up = jnp.square(jnp.maximum(_bdot(h2, wu_ref[...]), 0.0))
    x = x + _bdot(up.astype(BF16), wd_ref[...])
    h3 = _rmsnorm(x, g_ple_ref[...]).astype(BF16)
    gate = _sigmoid(_bdot(h3, wpg_ref[...]))
    x = x + gate * _bdot(p_ref[...].astype(BF16), wpl_ref[...])
    o_ref[...] = _rmsnorm(x, g_fin_ref[...])


def _post(x2, y_lru, y_att, g_lru, g_att, p2, w_bl, w_ba, w_o, g_mlp, w_up, w_down, g_ple, w_pg,
          w_pl, g_fin, tm):
    t, d = x2.shape
    tile = pl.BlockSpec((tm, d), lambda i: (i, 0))
    ptile = pl.BlockSpec((tm, p2.shape[1]), lambda i: (i, 0))
    row = _resident((1, d))
    square = _resident((d, d))
    return pl.pallas_call(
        _post_kernel,
        grid=(t // tm,),
        in_specs=[tile] * 5 + [ptile, square, square, square, row, _resident(w_up.shape),
                               _resident(w_down.shape), row, square, _resident(w_pl.shape), row],
        out_specs=tile,
        out_shape=jax.ShapeDtypeStruct((t, d), F32),
        compiler_params=pltpu.CompilerParams(
            dimension_semantics=("arbitrary",), vmem_limit_bytes=VMEM_LIMIT),
        name="post",
    )(x2, y_lru, y_att, g_lru, g_att, p2, w_bl, w_ba, w_o, g_mlp, w_up, w_down, g_ple, w_pg,
      w_pl, g_fin)


def kernel(x, p, norm_mix_g, w_in, conv_w, conv_b, w_rgate, b_rgate, w_igate, b_igate, lru_lambda, w_br_lru, w_br_att, w_out, norm_mlp_g, w_mlp_up, w_mlp_down, norm_ple_g, w_ple_gate, w_ple, norm_final_g):
    batch, seq, d = x.shape
    depth = w_in.shape[0]
    t = batch * seq
    x2 = x.reshape(t, d)
    for l in range(depth):
        q, k, v, g_lru, g_att, y_lru = _in_lru(
            x2, norm_mix_g[l].reshape(1, d), w_in[l].astype(BF16), conv_w[l], conv_b[l],
            w_rgate[l], b_rgate[l], w_igate[l], b_igate[l], lru_lambda[l], batch, ts=512)
        y_att = _attn(q, k, v, batch, tq=256)
        assert l == depth - 1, "deeper stacks need a post variant without the final norm"
        x2 = _post(x2, y_lru, y_att, g_lru, g_att, p[l].reshape(t, -1),
                   w_br_lru[l].astype(BF16), w_br_att[l].astype(BF16), w_out[l].astype(BF16),
                   norm_mlp_g[l].reshape(1, d), w_mlp_up[l].astype(BF16),
                   w_mlp_down[l].astype(BF16), norm_ple_g[l].reshape(1, d),
                   w_ple_gate[l].astype(BF16), w_ple[l].astype(BF16),
                   norm_final_g.reshape(1, d), tm=256)
    return x2.reshape(batch, seq, d)
```

```python
import jax
import jax.numpy as jnp
from jax import lax
from jax.experimental import pallas as pl
from jax.experimental.pallas import tpu as pltpu

F32 = jnp.float32
BF16 = jnp.bfloat16

NORM_EPS = 1e-6
LRU_C = 8.0
CONV_W = 4
HEAD_DIM = 64
SUBLANES = 8
MXU_DIM = 256
ATTN_LANES = 128
HEADS_PER_BLOCK = ATTN_LANES // HEAD_DIM

LOG2E = 1.4426950408889634
EXP_UNDERFLOW = 105.0
MASKED_LOGIT = -1e30

VMEM_LIMIT = 60 * 1024 * 1024


def _resident(shape):
    nd = len(shape)
    return pl.BlockSpec(shape, lambda *_: (0,) * nd, pipeline_mode=pl.Buffered(1))


def _rmsnorm(x, g):
    return x * lax.rsqrt(jnp.mean(x * x, axis=-1, keepdims=True) + NORM_EPS) * g


def _sigmoid(x):
    return 0.5 * jnp.tanh(0.5 * x) + 0.5


def _bdot(a, b):
    return jnp.dot(a, b, preferred_element_type=F32)


def _shift_rows(cur, prev_tail, back, row):
    rolled = pltpu.roll(cur, back, axis=0)
    first = jnp.where(row < back, pltpu.roll(prev_tail, back, axis=0), rolled[0:SUBLANES])
    return jnp.concatenate([first, rolled[SUBLANES:]], axis=0)


def _in_lru_kernel(x_ref, g_ref, w_ref, cw_ref, cb_ref, wr_ref, br_ref, wi_ref, bi_ref, lam_ref,
                   q_ref, k_ref, v_ref, gl_ref, ga_ref, y_ref,
                   tail, a_scr, u_scr, h_scr, hprev):
    ts, d = x_ref.shape

    @pl.when(pl.program_id(1) == 0)
    def _():
        tail[...] = jnp.zeros(tail.shape, F32)
        hprev[...] = jnp.zeros(hprev.shape, F32)

    h = _rmsnorm(x_ref[...], g_ref[...]).astype(BF16)

    def proj(n):
        return _bdot(h, w_ref[:, n * d:(n + 1) * d])

    outs = {2: q_ref, 3: k_ref, 4: v_ref, 5: gl_ref, 6: ga_ref}

    def emit(n):
        outs[n][...] = proj(n).astype(outs[n].dtype)

    ux = proj(0)
    ug = proj(1)
    row = lax.broadcasted_iota(jnp.int32, (SUBLANES, d), 0)
    prev_tail = tail[...]
    c = cb_ref[...]
    for k in range(CONV_W):
        back = CONV_W - 1 - k
        xs = ux if back == 0 else _shift_rows(ux, prev_tail, back, row)
        c = c + cw_ref[k:k + 1, :] * xs
    tail[...] = ux[ts - SUBLANES:, :]
    emit(2)

    cb = c.astype(BF16)
    gr, gi = [], []
    for j in range(d // MXU_DIM):
        sl = slice(j * MXU_DIM, (j + 1) * MXU_DIM)
        gr.append(_bdot(cb[:, sl], wr_ref[j]))
        gi.append(_bdot(cb[:, sl], wi_ref[j]))
    r = _sigmoid(jnp.concatenate(gr, axis=1) + br_ref[...])
    ig = _sigmoid(jnp.concatenate(gi, axis=1) + bi_ref[...])
    emit(3)

    neg_lam = -lam_ref[...]
    softplus = jnp.maximum(neg_lam, 0.0) + jnp.log1p(jnp.exp(-jnp.abs(neg_lam)))
    log_a = (-LRU_C) * r * softplus
    a = jnp.exp(log_a)
    one_m_a2 = jnp.tanh(-log_a) * (a * a + 1.0)
    mult = jnp.where(one_m_a2 > 0.0, one_m_a2 * lax.rsqrt(one_m_a2), 0.0)
    a_scr[...] = a
    u_scr[...] = mult * (ig * c)
    emit(4)

    h_in = hprev[...]
    for g0 in range(0, ts, SUBLANES):
        if g0 == ts // 2:
            emit(5)
        a8 = a_scr[g0:g0 + SUBLANES, :]
        u8 = u_scr[g0:g0 + SUBLANES, :]
        for dist in (1, 2, 4):
            keep = row >= dist
            u_sh = pltpu.roll(u8, dist, axis=0)
            a_sh = pltpu.roll(a8, dist, axis=0)
            u8 = jnp.where(keep, a8 * u_sh + u8, u8)
            a8 = jnp.where(keep, a8 * a_sh, a8)
        h8 = a8 * h_in + u8
        h_scr[g0:g0 + SUBLANES, :] = h8
        h_in = h8[SUBLANES - 1:SUBLANES, :]
    hprev[...] = h_in
    emit(6)

    y_ref[...] = (h_scr[...] * jax.nn.gelu(ug, approximate=True)).astype(y_ref.dtype)


def _block_diag(wb, group):
    n, c, _ = wb.shape
    wg = wb.reshape(n // group, group, c, c)
    eye = jnp.eye(group, dtype=wb.dtype)
    dense = wg[:, :, :, None, :] * eye[None, :, None, :, None]
    return dense.reshape(n // group, group * c, group * c)


def _in_lru(x2, g, w_bf, conv_w, conv_b, w_r, b_r, w_i, b_i, lam, batch, ts):
    t, d = x2.shape
    nt = t // batch // ts
    group = MXU_DIM // w_r.shape[1]
    wr = _block_diag(w_r, group).astype(BF16)
    wi = _block_diag(w_i, group).astype(BF16)
    tile = pl.BlockSpec((ts, d), lambda b, i: (b * nt + i, 0))
    row = _resident((1, d))
    dtypes = [BF16, BF16, BF16, F32, F32, BF16]
    return pl.pallas_call(
        _in_lru_kernel,
        grid=(batch, nt),
        in_specs=[tile, row, _resident(w_bf.shape), _resident(conv_w.shape), row,
                  _resident(wr.shape), row, _resident(wi.shape), row, row],
        out_specs=[tile] * len(dtypes),
        out_shape=[jax.ShapeDtypeStruct((t, d), dt) for dt in dtypes],
        scratch_shapes=[pltpu.VMEM((SUBLANES, d), F32), pltpu.VMEM((ts, d), F32),
                        pltpu.VMEM((ts, d), F32), pltpu.VMEM((ts, d), F32),
                        pltpu.VMEM((1, d), F32)],
        compiler_params=pltpu.CompilerParams(
            dimension_semantics=("arbitrary", "arbitrary"), vmem_limit_bytes=VMEM_LIMIT),
        name="in_lru",
    )(x2, g, w_bf, conv_w, conv_b.reshape(1, d), wr, b_r.reshape(1, d), wi, b_i.reshape(1, d),
      lam.reshape(1, d))


def _softplus(z):
    return jnp.maximum(z, 0.0) + jnp.log(1.0 + jnp.exp2(jnp.abs(z) * (-LOG2E)))


def _attn_kernel(qp_ref, qn_ref, k_ref, v_ref, tri_ref, o_ref, z_scr, att_scr, carry_scr):
    tq = qp_ref.shape[0]
    tk = tri_ref.shape[1]
    i = pl.program_id(2)
    nq = pl.num_programs(2) - 1
    lane = lax.broadcasted_iota(jnp.int32, (tq, ATTN_LANES), 1)
    tri = tri_ref[...]

    def stack_heads(q):
        q = q * (HEAD_DIM ** -0.5)
        zero = jnp.zeros_like(q)
        return jnp.concatenate(
            [jnp.where(lane // HEAD_DIM == h, q, zero) for h in range(HEADS_PER_BLOCK)], axis=0)

    def tile_rows(j):
        return pl.ds(pl.multiple_of(j * tk, tk), tk)

    def scores(q2, k_tiles):
        return lax.dot_general(q2, k_tiles, (((1,), (1,)), ((), ())), preferred_element_type=F32)

    def first_scores(q2, j):
        prev = jnp.maximum(j - 1, 0)
        return scores(q2, jnp.concatenate([k_ref[tile_rows(j), :], k_ref[tile_rows(prev), :]],
                                          axis=0))

    def later_sums(z):
        return _bdot(_softplus(z).astype(BF16), tri)

    @pl.when(i == 0)
    def _():
        z_scr[...] = first_scores(stack_heads(qp_ref[...]), 0)
        att_scr[...] = jnp.zeros(att_scr.shape, BF16)
        carry_scr[...] = jnp.full(carry_scr.shape, 2.0 * EXP_UNDERFLOW, F32)

    done = jnp.maximum(i - 1, 0)
    carry_done = carry_scr[...]
    v_prev = v_ref[tile_rows(jnp.maximum(done - 1, 0)), :]
    v_both = jnp.concatenate(
        [v_ref[tile_rows(done), :], jnp.where(done > 0, v_prev, jnp.zeros_like(v_prev))], axis=0)
    acc = _bdot(att_scr[...], v_both)

    z = z_scr[...]
    z_scr[...] = first_scores(stack_heads(qn_ref[...]), jnp.minimum(i + 1, nq - 1))
    rq = lax.broadcasted_iota(jnp.int32, (HEADS_PER_BLOCK * tq, tk), 0) % tq
    ck = lax.broadcasted_iota(jnp.int32, (HEADS_PER_BLOCK * tq, tk), 1)
    z_diag = jnp.where(ck < rq, z[:, :tk], MASKED_LOGIT)
    z_prev = z[:, tk:]
    within = later_sums(jnp.concatenate([z_diag, z_prev], axis=0))
    within_diag, within_prev = within[:HEADS_PER_BLOCK * tq], within[HEADS_PER_BLOCK * tq:]
    carry = within_diag[:, 0:1]
    att_scr[...] = jnp.concatenate(
        [jnp.exp(z_diag - within_diag), jnp.exp(z_prev - (within_prev + carry))],
        axis=1).astype(BF16)
    carry_scr[...] = carry + within_prev[:, 0:1]

    def cond(state):
        j, low, _, _ = state
        return jnp.logical_and(j >= 0, low < EXP_UNDERFLOW)

    def body(state):
        j, _, acc, carry = state
        z = scores(stack_heads(qp_ref[...]), k_ref[tile_rows(j), :])
        within = later_sums(z)
        att = jnp.exp(z - (within + carry))
        acc = acc + _bdot(att.astype(BF16), v_ref[tile_rows(j), :])
        carry = carry + within[:, 0:1]
        return j - 1, jnp.min(carry), acc, carry

    _, _, acc, _ = lax.while_loop(cond, body, (i - 3, jnp.min(carry_done), acc, carry_done))
    out = acc[:tq]
    for h in range(1, HEADS_PER_BLOCK):
        out = jnp.where(lane // HEAD_DIM == h, acc[h * tq:(h + 1) * tq], out)
    o_ref[...] = out.astype(o_ref.dtype)


def _attn(q, k, v, batch, tq):
    t, width = q.shape
    seq = t // batch
    nq = seq // tq
    tk = tq
    later = jnp.arange(tk)[:, None] >= jnp.arange(tk)[None, :]

    def q_tile(shift):
        return pl.BlockSpec(
            (tq, ATTN_LANES), lambda b, hp, i: (b * nq + jnp.clip(i + shift, 0, nq - 1), hp))

    kv_spec = pl.BlockSpec((seq, ATTN_LANES), lambda b, hp, i: (b, hp))
    rows = HEADS_PER_BLOCK * tq
    return pl.pallas_call(
        _attn_kernel,
        grid=(batch, width // ATTN_LANES, nq + 1),
        in_specs=[q_tile(-1), q_tile(1), kv_spec, kv_spec, _resident((tk, tk))],
        out_specs=q_tile(-1),
        out_shape=jax.ShapeDtypeStruct((t, width), BF16),
        scratch_shapes=[pltpu.VMEM((rows, 2 * tk), F32), pltpu.VMEM((rows, 2 * tk), BF16),
                        pltpu.VMEM((rows, 1), F32)],
        compiler_params=pltpu.CompilerParams(
            dimension_semantics=("arbitrary", "arbitrary", "arbitrary"),
            vmem_limit_bytes=VMEM_LIMIT),
        name="attn",
    )(q, q, k, v, later.astype(BF16))


def _post_kernel(x_ref, yl_ref, ya_ref, gl_ref, ga_ref, p_ref, wbl_ref, wba_ref, wo_ref,
                 g_mlp_ref, wu_ref, wd_ref, g_ple_ref, wpg_ref, wpl_ref, g_fin_ref, o_ref):
    merged = (_sigmoid(gl_ref[...]) * _bdot(yl_ref[...], wbl_ref[...])
              + _sigmoid(ga_ref[...]) * _bdot(ya_ref[...], wba_ref[...]))
    x = x_ref[...] + _bdot(merged.astype(BF16), wo_ref[...])
    h2 = _rmsnorm(x, g_mlp_ref[...]).astype(BF16)
    up = jnp.square(jnp.maximum(_bdot(h2, wu_ref[...]), 0.0))
    x = x + _bdot(up.astype(BF16), wd_ref[...])
    h3 = _rmsnorm(x, g_ple_ref[...]).astype(BF16)
    gate = _sigmoid(_bdot(h3, wpg_ref[...]))
    x = x + gate * _bdot(p_ref[...].astype(BF16), wpl_ref[...])
    o_ref[...] = _rmsnorm(x, g_fin_ref[...])


def _post(x2, y_lru, y_att, g_lru, g_att, p2, w_bl, w_ba, w_o, g_mlp, w_up, w_down, g_ple, w_pg,
          w_pl, g_fin, tm):
    t, d = x2.shape
    tile = pl.BlockSpec((tm, d), lambda i: (i, 0))
    ptile = pl.BlockSpec((tm, p2.shape[1]), lambda i: (i, 0))
    row = _resident((1, d))
    square = _resident((d, d))
    return pl.pallas_call(
        _post_kernel,
        grid=(t // tm,),
        in_specs=[tile] * 5 + [ptile, square, square, square, row, _resident(w_up.shape),
                               _resident(w_down.shape), row, square, _resident(w_pl.shape), row],
        out_specs=tile,
        out_shape=jax.ShapeDtypeStruct((t, d), F32),
        compiler_params=pltpu.CompilerParams(
            dimension_semantics=("arbitrary",), vmem_limit_bytes=VMEM_LIMIT),
        name="post",
    )(x2, y_lru, y_att, g_lru, g_att, p2, w_bl, w_ba, w_o, g_mlp, w_up, w_down, g_ple, w_pg,
      w_pl, g_fin)


def kernel(x, p, norm_mix_g, w_in, conv_w, conv_b, w_rgate, b_rgate, w_igate, b_igate, lru_lambda, w_br_lru, w_br_att, w_out, norm_mlp_g, w_mlp_up, w_mlp_down, norm_ple_g, w_ple_gate, w_ple, norm_final_g):
    batch, seq, d = x.shape
    depth = w_in.shape[0]
    t = batch * seq
    x2 = x.reshape(t, d)
    for l in range(depth):
        q, k, v, g_lru, g_att, y_lru = _in_lru(
            x2, norm_mix_g[l].reshape(1, d), w_in[l].astype(BF16), conv_w[l], conv_b[l],
            w_rgate[l], b_rgate[l], w_igate[l], b_igate[l], lru_lambda[l], batch, ts=512)
        y_att = _attn(q, k, v, batch, tq=256)
        assert l == depth - 1, "deeper stacks need a post variant without the final norm"
        x2 = _post(x2, y_lru, y_att, g_lru, g_att, p[l].reshape(t, -1),
                   w_br_lru[l].astype(BF16), w_br_att[l].astype(BF16), w_out[l].astype(BF16),
                   norm_mlp_g[l].reshape(1, d), w_mlp_up[l].astype(BF16),
                   w_mlp_down[l].astype(BF16), norm_ple_g[l].reshape(1, d),
                   w_ple_gate[l].astype(BF16), w_ple[l].astype(BF16),
                   norm_final_g.reshape(1, d), tm=512)
    return x2.reshape(batch, seq, d)
```

```python
import jax
import jax.numpy as jnp
from jax import lax
from jax.experimental import pallas as pl
from jax.experimental.pallas import tpu as pltpu

F32 = jnp.float32
BF16 = jnp.bfloat16

NORM_EPS = 1e-6
LRU_C = 8.0
CONV_W = 4
HEAD_DIM = 64
SUBLANES = 8
MXU_DIM = 256
ATTN_LANES = 128
HEADS_PER_BLOCK = ATTN_LANES // HEAD_DIM

LOG2E = 1.4426950408889634
EXP_UNDERFLOW = 105.0
MASKED_LOGIT = -1e30

VMEM_LIMIT = 60 * 1024 * 1024


def _resident(shape):
    nd = len(shape)
    return pl.BlockSpec(shape, lambda *_: (0,) * nd, pipeline_mode=pl.Buffered(1))


def _rmsnorm(x, g):
    return x * lax.rsqrt(jnp.mean(x * x, axis=-1, keepdims=True) + NORM_EPS) * g


def _sigmoid(x):
    return 0.5 * jnp.tanh(0.5 * x) + 0.5


def _bdot(a, b):
    return jnp.dot(a, b, preferred_element_type=F32)


def _shift_rows(cur, prev_tail, back, row):
    rolled = pltpu.roll(cur, back, axis=0)
    first = jnp.where(row < back, pltpu.roll(prev_tail, back, axis=0), rolled[0:SUBLANES])
    return jnp.concatenate([first, rolled[SUBLANES:]], axis=0)


def _in_lru_kernel(x_ref, g_ref, w_ref, cw_ref, cb_ref, wr_ref, br_ref, wi_ref, bi_ref, lam_ref,
                   q_ref, k_ref, v_ref, gl_ref, ga_ref, y_ref,
                   tail, a_scr, u_scr, h_scr, hprev):
    ts, d = x_ref.shape

    @pl.when(pl.program_id(1) == 0)
    def _():
        tail[...] = jnp.zeros(tail.shape, F32)
        hprev[...] = jnp.zeros(hprev.shape, F32)

    h = _rmsnorm(x_ref[...], g_ref[...]).astype(BF16)

    def proj(n):
        return _bdot(h, w_ref[:, n * d:(n + 1) * d])

    outs = {2: q_ref, 3: k_ref, 4: v_ref, 5: gl_ref, 6: ga_ref}

    def emit(n):
        outs[n][...] = proj(n).astype(outs[n].dtype)

    ux = proj(0)
    ug = proj(1)
    row = lax.broadcasted_iota(jnp.int32, (SUBLANES, d), 0)
    prev_tail = tail[...]
    c = cb_ref[...]
    for k in range(CONV_W):
        back = CONV_W - 1 - k
        xs = ux if back == 0 else _shift_rows(ux, prev_tail, back, row)
        c = c + cw_ref[k:k + 1, :] * xs
    tail[...] = ux[ts - SUBLANES:, :]
    emit(2)

    cb = c.astype(BF16)
    gr, gi = [], []
    for j in range(d // MXU_DIM):
        sl = slice(j * MXU_DIM, (j + 1) * MXU_DIM)
        gr.append(_bdot(cb[:, sl], wr_ref[j]))
        gi.append(_bdot(cb[:, sl], wi_ref[j]))
    r = _sigmoid(jnp.concatenate(gr, axis=1) + br_ref[...])
    ig = _sigmoid(jnp.concatenate(gi, axis=1) + bi_ref[...])
    emit(3)

    neg_lam = -lam_ref[...]
    softplus = jnp.maximum(neg_lam, 0.0) + jnp.log1p(jnp.exp(-jnp.abs(neg_lam)))
    log_a = (-LRU_C) * r * softplus
    a = jnp.exp(log_a)
    one_m_a2 = jnp.tanh(-log_a) * (a * a + 1.0)
    mult = jnp.where(one_m_a2 > 0.0, one_m_a2 * lax.rsqrt(one_m_a2), 0.0)
    a_scr[...] = a
    u_scr[...] = mult * (ig * c)
    emit(4)

    h_in = hprev[...]
    for g0 in range(0, ts, SUBLANES):
        if g0 == ts // 2:
            emit(5)
        a8 = a_scr[g0:g0 + SUBLANES, :]
        u8 = u_scr[g0:g0 + SUBLANES, :]
        for dist in (1, 2, 4):
            keep = row >= dist
            u_sh = pltpu.roll(u8, dist, axis=0)
            a_sh = pltpu.roll(a8, dist, axis=0)
            u8 = jnp.where(keep, a8 * u_sh + u8, u8)
            a8 = jnp.where(keep, a8 * a_sh, a8)
        h8 = a8 * h_in + u8
        h_scr[g0:g0 + SUBLANES, :] = h8
        h_in = h8[SUBLANES - 1:SUBLANES, :]
    hprev[...] = h_in
    emit(6)

    y_ref[...] = (h_scr[...] * jax.nn.gelu(ug, approximate=True)).astype(y_ref.dtype)


def _block_diag(wb, group):
    n, c, _ = wb.shape
    wg = wb.reshape(n // group, group, c, c)
    eye = jnp.eye(group, dtype=wb.dtype)
    dense = wg[:, :, :, None, :] * eye[None, :, None, :, None]
    return dense.reshape(n // group, group * c, group * c)


def _in_lru(x2, g, w_bf, conv_w, conv_b, w_r, b_r, w_i, b_i, lam, batch, ts):
    t, d = x2.shape
    nt = t // batch // ts
    group = MXU_DIM // w_r.shape[1]
    wr = _block_diag(w_r, group).astype(BF16)
    wi = _block_diag(w_i, group).astype(BF16)
    tile = pl.BlockSpec((ts, d), lambda b, i: (b * nt + i, 0))
    row = _resident((1, d))
    dtypes = [BF16, BF16, BF16, F32, F32, BF16]
    return pl.pallas_call(
        _in_lru_kernel,
        grid=(batch, nt),
        in_specs=[tile, row, _resident(w_bf.shape), _resident(conv_w.shape), row,
                  _resident(wr.shape), row, _resident(wi.shape), row, row],
        out_specs=[tile] * len(dtypes),
        out_shape=[jax.ShapeDtypeStruct((t, d), dt) for dt in dtypes],
        scratch_shapes=[pltpu.VMEM((SUBLANES, d), F32), pltpu.VMEM((ts, d), F32),
                        pltpu.VMEM((ts, d), F32), pltpu.VMEM((ts, d), F32),
                        pltpu.VMEM((1, d), F32)],
        compiler_params=pltpu.CompilerParams(
            dimension_semantics=("arbitrary", "arbitrary"), vmem_limit_bytes=VMEM_LIMIT),
        name="in_lru",
    )(x2, g, w_bf, conv_w, conv_b.reshape(1, d), wr, b_r.reshape(1, d), wi, b_i.reshape(1, d),
      lam.reshape(1, d))


def _softplus(z):
    return jnp.maximum(z, 0.0) + jnp.log(1.0 + jnp.exp2(jnp.abs(z) * (-LOG2E)))


LANE_BLOCKS = 2


def _attn_kernel(qp_ref, qn_ref, k_ref, v_ref, tri_ref, o_ref, z_scr, att_scr, carry_scr):
    tq = qp_ref.shape[0]
    tk = tri_ref.shape[1]
    i = pl.program_id(2)
    nq = pl.num_programs(2) - 1
    lane = lax.broadcasted_iota(jnp.int32, (tq, ATTN_LANES), 1)
    tri = tri_ref[...]
    done = jnp.maximum(i - 1, 0)

    def stack_heads(q):
        q = q * (HEAD_DIM ** -0.5)
        zero = jnp.zeros_like(q)
        return jnp.concatenate(
            [jnp.where(lane // HEAD_DIM == h, q, zero) for h in range(HEADS_PER_BLOCK)], axis=0)

    def tile_rows(j):
        return pl.ds(pl.multiple_of(j * tk, tk), tk)

    def scores(q2, k_tiles):
        return lax.dot_general(q2, k_tiles, (((1,), (1,)), ((), ())), preferred_element_type=F32)

    def later_sums(z):
        return _bdot(_softplus(z).astype(BF16), tri)

    def first_scores(q2, j, lanes):
        prev = jnp.maximum(j - 1, 0)
        return scores(q2, jnp.concatenate(
            [k_ref[tile_rows(j), lanes], k_ref[tile_rows(prev), lanes]], axis=0))

    @pl.when(i == 0)
    def _():
        for lb in range(LANE_BLOCKS):
            lanes = slice(lb * ATTN_LANES, (lb + 1) * ATTN_LANES)
            z_scr[lb] = first_scores(stack_heads(qp_ref[:, lanes]), 0, lanes)
        att_scr[...] = jnp.zeros(att_scr.shape, BF16)
        carry_scr[...] = jnp.full(carry_scr.shape, 2.0 * EXP_UNDERFLOW, F32)

    def first_steps(lb):
        lanes = slice(lb * ATTN_LANES, (lb + 1) * ATTN_LANES)

        carry_done = carry_scr[lb]
        v_prev = v_ref[tile_rows(jnp.maximum(done - 1, 0)), lanes]
        v_both = jnp.concatenate(
            [v_ref[tile_rows(done), lanes], jnp.where(done > 0, v_prev, jnp.zeros_like(v_prev))],
            axis=0)
        acc = _bdot(att_scr[lb], v_both)

        z = z_scr[lb]
        z_scr[lb] = first_scores(stack_heads(qn_ref[:, lanes]), jnp.minimum(i + 1, nq - 1),
                                 lanes)
        rq = lax.broadcasted_iota(jnp.int32, (HEADS_PER_BLOCK * tq, tk), 0) % tq
        ck = lax.broadcasted_iota(jnp.int32, (HEADS_PER_BLOCK * tq, tk), 1)
        z_diag = jnp.where(ck < rq, z[:, :tk], MASKED_LOGIT)
        z_prev = z[:, tk:]
        within = later_sums(jnp.concatenate([z_diag, z_prev], axis=0))
        within_diag = within[:HEADS_PER_BLOCK * tq]
        within_prev = within[HEADS_PER_BLOCK * tq:]
        carry = within_diag[:, 0:1]
        att_scr[lb] = jnp.concatenate(
            [jnp.exp(z_diag - within_diag), jnp.exp(z_prev - (within_prev + carry))],
            axis=1).astype(BF16)
        carry_scr[lb] = carry + within_prev[:, 0:1]
        return acc, carry_done

    def earlier_tiles(lb, acc, carry_done):
        lanes = slice(lb * ATTN_LANES, (lb + 1) * ATTN_LANES)

        def cond(state):
            j, low, _, _ = state
            return jnp.logical_and(j >= 0, low < EXP_UNDERFLOW)

        def body(state):
            j, _, acc, carry = state
            z = scores(stack_heads(qp_ref[:, lanes]), k_ref[tile_rows(j), lanes])
            within = later_sums(z)
            att = jnp.exp(z - (within + carry))
            acc = acc + _bdot(att.astype(BF16), v_ref[tile_rows(j), lanes])
            carry = carry + within[:, 0:1]
            return j - 1, jnp.min(carry), acc, carry

        _, _, acc, _ = lax.while_loop(cond, body, (i - 3, jnp.min(carry_done), acc, carry_done))
        out = acc[:tq]
        for h in range(1, HEADS_PER_BLOCK):
            out = jnp.where(lane // HEAD_DIM == h, acc[h * tq:(h + 1) * tq], out)
        o_ref[:, lanes] = out.astype(o_ref.dtype)

    started = [first_steps(lb) for lb in range(LANE_BLOCKS)]
    for lb, (acc, carry_done) in enumerate(started):
        earlier_tiles(lb, acc, carry_done)


def _attn(q, k, v, batch, tq):
    t, width = q.shape
    seq = t // batch
    nq = seq // tq
    tk = tq
    group = LANE_BLOCKS * ATTN_LANES
    later = jnp.arange(tk)[:, None] >= jnp.arange(tk)[None, :]

    def q_tile(shift):
        return pl.BlockSpec(
            (tq, group), lambda b, hp, i: (b * nq + jnp.clip(i + shift, 0, nq - 1), hp))

    kv_spec = pl.BlockSpec((seq, group), lambda b, hp, i: (b, hp))
    rows = HEADS_PER_BLOCK * tq
    return pl.pallas_call(
        _attn_kernel,
        grid=(batch, width // group, nq + 1),
        in_specs=[q_tile(-1), q_tile(1), kv_spec, kv_spec, _resident((tk, tk))],
        out_specs=q_tile(-1),
        out_shape=jax.ShapeDtypeStruct((t, width), BF16),
        scratch_shapes=[pltpu.VMEM((LANE_BLOCKS, rows, 2 * tk), F32),
                        pltpu.VMEM((LANE_BLOCKS, rows, 2 * tk), BF16),
                        pltpu.VMEM((LANE_BLOCKS, rows, 1), F32)],
        compiler_params=pltpu.CompilerParams(
            dimension_semantics=("arbitrary", "arbitrary", "arbitrary"),
            vmem_limit_bytes=VMEM_LIMIT),
        name="attn",
    )(q, q, k, v, later.astype(BF16))


def _post_kernel(x_ref, yl_ref, ya_ref, gl_ref, ga_ref, p_ref, wbl_ref, wba_ref, wo_ref,
                 g_mlp_ref, wu_ref, wd_ref, g_ple_ref, wpg_ref, wpl_ref, g_fin_ref, o_ref):
    merged = (_sigmoid(gl_ref[...]) * _bdot(yl_ref[...], wbl_ref[...])
              + _sigmoid(ga_ref[...]) * _bdot(ya_ref[...], wba_ref[...]))
    x = x_ref[...] + _bdot(merged.astype(BF16), wo_ref[...])
    h2 = _rmsnorm(x, g_mlp_ref[...]).astype(BF16)
    up = jnp.square(jnp.maximum(_bdot(h2, wu_ref[...]), 0.0))
    x = x + _bdot(up.astype(BF16), wd_ref[...])
    h3 = _rmsnorm(x, g_ple_ref[...]).astype(BF16)
    gate = _sigmoid(_bdot(h3, wpg_ref[...]))
    x = x + gate * _bdot(p_ref[...].astype(BF16), wpl_ref[...])
    o_ref[...] = _rmsnorm(x, g_fin_ref[...])


def _post(x2, y_lru, y_att, g_lru, g_att, p2, w_bl, w_ba, w_o, g_mlp, w_up, w_down, g_ple, w_pg,
          w_pl, g_fin, tm):
    t, d = x2.shape
    tile = pl.BlockSpec((tm, d), lambda i: (i, 0))
    ptile = pl.BlockSpec((tm, p2.shape[1]), lambda i: (i, 0))
    row = _resident((1, d))
    square = _resident((d, d))
    return pl.pallas_call(
        _post_kernel,
        grid=(t // tm,),
        in_specs=[tile] * 5 + [ptile, square, square, square, row, _resident(w_up.shape),
                               _resident(w_down.shape), row, square, _resident(w_pl.shape), row],
        out_specs=tile,
        out_shape=jax.ShapeDtypeStruct((t, d), F32),
        compiler_params=pltpu.CompilerParams(
            dimension_semantics=("arbitrary",), vmem_limit_bytes=VMEM_LIMIT),
        name="post",
    )(x2, y_lru, y_att, g_lru, g_att, p2, w_bl, w_ba, w_o, g_mlp, w_up, w_down, g_ple, w_pg,
      w_pl, g_fin)


def kernel(x, p, norm_mix_g, w_in, conv_w, conv_b, w_rgate, b_rgate, w_igate, b_igate, lru_lambda, w_br_lru, w_br_att, w_out, norm_mlp_g, w_mlp_up, w_mlp_down, norm_ple_g, w_ple_gate, w_ple, norm_final_g):
    batch, seq, d = x.shape
    depth = w_in.shape[0]
    t = batch * seq
    x2 = x.reshape(t, d)
    for l in range(depth):
        q, k, v, g_lru, g_att, y_lru = _in_lru(
            x2, norm_mix_g[l].reshape(1, d), w_in[l].astype(BF16), conv_w[l], conv_b[l],
            w_rgate[l], b_rgate[l], w_igate[l], b_igate[l], lru_lambda[l], batch, ts=512)
        y_att = _attn(q, k, v, batch, tq=256)
        assert l == depth - 1, "deeper stacks need a post variant without the final norm"
        x2 = _post(x2, y_lru, y_att, g_lru, g_att, p[l].reshape(t, -1),
                   w_br_lru[l].astype(BF16), w_br_att[l].astype(BF16), w_out[l].astype(BF16),
                   norm_mlp_g[l].reshape(1, d), w_mlp_up[l].astype(BF16),
                   w_mlp_down[l].astype(BF16), norm_ple_g[l].reshape(1, d),
                   w_ple_gate[l].astype(BF16), w_ple[l].astype(BF16),
                   norm_final_g.reshape(1, d), tm=512)
    return x2.reshape(batch, seq, d)
```

```python
import jax
import jax.numpy as jnp
from jax import lax
from jax.experimental import pallas as pl
from jax.experimental.pallas import tpu as pltpu

F32 = jnp.float32
BF16 = jnp.bfloat16

NORM_EPS = 1e-6
LRU_C = 8.0
CONV_W = 4
HEAD_DIM = 64
SUBLANES = 8
MXU_DIM = 256
ATTN_LANES = 128
HEADS_PER_BLOCK = ATTN_LANES // HEAD_DIM

LOG2E = 1.4426950408889634
EXP_UNDERFLOW = 105.0
MASKED_LOGIT = -1e30

VMEM_LIMIT = 60 * 1024 * 1024


def _resident(shape):
    nd = len(shape)
    return pl.BlockSpec(shape, lambda *_: (0,) * nd, pipeline_mode=pl.Buffered(1))


def _rmsnorm(x, g):
    return x * lax.rsqrt(jnp.mean(x * x, axis=-1, keepdims=True) + NORM_EPS) * g


def _sigmoid(x):
    return 0.5 * jnp.tanh(0.5 * x) + 0.5


def _bdot(a, b):
    return jnp.dot(a, b, preferred_element_type=F32)


def _shift_rows(cur, prev_tail, back, row):
    rolled = pltpu.roll(cur, back, axis=0)
    first = jnp.where(row < back, pltpu.roll(prev_tail, back, axis=0), rolled[0:SUBLANES])
    return jnp.concatenate([first, rolled[SUBLANES:]], axis=0)


def _in_lru_kernel(x_ref, g_ref, w_ref, cw_ref, cb_ref, wr_ref, br_ref, wi_ref, bi_ref, lam_ref,
                   q_ref, k_ref, v_ref, gl_ref, ga_ref, y_ref,
                   tail, a_scr, u_scr, h_scr, hprev):
    ts, d = x_ref.shape

    @pl.when(pl.program_id(1) == 0)
    def _():
        tail[...] = jnp.zeros(tail.shape, F32)
        hprev[...] = jnp.zeros(hprev.shape, F32)

    h = _rmsnorm(x_ref[...], g_ref[...]).astype(BF16)

    def proj(n):
        return _bdot(h, w_ref[:, n * d:(n + 1) * d])

    outs = {2: q_ref, 3: k_ref, 4: v_ref, 5: gl_ref, 6: ga_ref}

    def emit(n):
        outs[n][...] = proj(n).astype(outs[n].dtype)

    ux = proj(0)
    ug = proj(1)
    row = lax.broadcasted_iota(jnp.int32, (SUBLANES, d), 0)
    prev_tail = tail[...]
    c = cb_ref[...]
    for k in range(CONV_W):
        back = CONV_W - 1 - k
        xs = ux if back == 0 else _shift_rows(ux, prev_tail, back, row)
        c = c + cw_ref[k:k + 1, :] * xs
    tail[...] = ux[ts - SUBLANES:, :]
    emit(2)

    cb = c.astype(BF16)
    gr, gi = [], []
    for j in range(d // MXU_DIM):
        sl = slice(j * MXU_DIM, (j + 1) * MXU_DIM)
        gr.append(_bdot(cb[:, sl], wr_ref[j]))
        gi.append(_bdot(cb[:, sl], wi_ref[j]))
    r = _sigmoid(jnp.concatenate(gr, axis=1) + br_ref[...])
    ig = _sigmoid(jnp.concatenate(gi, axis=1) + bi_ref[...])
    emit(3)

    neg_lam = -lam_ref[...]
    softplus = jnp.maximum(neg_lam, 0.0) + jnp.log1p(jnp.exp(-jnp.abs(neg_lam)))
    log_a = (-LRU_C) * r * softplus
    a = jnp.exp(log_a)
    one_m_a2 = jnp.tanh(-log_a) * (a * a + 1.0)
    mult = jnp.where(one_m_a2 > 0.0, one_m_a2 * lax.rsqrt(one_m_a2), 0.0)
    a_scr[...] = a
    u_scr[...] = mult * (ig * c)
    emit(4)

    h_in = hprev[...]
    for g0 in range(0, ts, SUBLANES):
        if g0 == ts // 2:
            emit(5)
        a8 = a_scr[g0:g0 + SUBLANES, :]
        u8 = u_scr[g0:g0 + SUBLANES, :]
        for dist in (1, 2, 4):
            keep = row >= dist
            u_sh = pltpu.roll(u8, dist, axis=0)
            a_sh = pltpu.roll(a8, dist, axis=0)
            u8 = jnp.where(keep, a8 * u_sh + u8, u8)
            a8 = jnp.where(keep, a8 * a_sh, a8)
        h8 = a8 * h_in + u8
        h_scr[g0:g0 + SUBLANES, :] = h8
        h_in = h8[SUBLANES - 1:SUBLANES, :]
    hprev[...] = h_in
    emit(6)

    y_ref[...] = (h_scr[...] * jax.nn.gelu(ug, approximate=True)).astype(y_ref.dtype)


def _block_diag(wb, group):
    n, c, _ = wb.shape
    wg = wb.reshape(n // group, group, c, c)
    eye = jnp.eye(group, dtype=wb.dtype)
    dense = wg[:, :, :, None, :] * eye[None, :, None, :, None]
    return dense.reshape(n // group, group * c, group * c)


def _in_lru(x2, g, w_bf, conv_w, conv_b, w_r, b_r, w_i, b_i, lam, batch, ts):
    t, d = x2.shape
    nt = t // batch // ts
    group = MXU_DIM // w_r.shape[1]
    wr = _block_diag(w_r, group).astype(BF16)
    wi = _block_diag(w_i, group).astype(BF16)
    tile = pl.BlockSpec((ts, d), lambda b, i: (b * nt + i, 0))
    row = _resident((1, d))
    dtypes = [BF16, BF16, BF16, F32, F32, BF16]
    return pl.pallas_call(
        _in_lru_kernel,
        grid=(batch, nt),
        in_specs=[tile, row, _resident(w_bf.shape), _resident(conv_w.shape), row,
                  _resident(wr.shape), row, _resident(wi.shape), row, row],
        out_specs=[tile] * len(dtypes),
        out_shape=[jax.ShapeDtypeStruct((t, d), dt) for dt in dtypes],
        scratch_shapes=[pltpu.VMEM((SUBLANES, d), F32), pltpu.VMEM((ts, d), F32),
                        pltpu.VMEM((ts, d), F32), pltpu.VMEM((ts, d), F32),
                        pltpu.VMEM((1, d), F32)],
        compiler_params=pltpu.CompilerParams(
            dimension_semantics=("arbitrary", "arbitrary"), vmem_limit_bytes=VMEM_LIMIT),
        name="in_lru",
    )(x2, g, w_bf, conv_w, conv_b.reshape(1, d), wr, b_r.reshape(1, d), wi, b_i.reshape(1, d),
      lam.reshape(1, d))


def _softplus(z):
    return jnp.maximum(z, 0.0) + jnp.log(1.0 + jnp.exp2(jnp.abs(z) * (-LOG2E)))


LANE_BLOCKS = 4


def _attn_kernel(qp_ref, qn_ref, k_ref, v_ref, tri_ref, o_ref, z_scr, att_scr, carry_scr):
    tq = qp_ref.shape[0]
    tk = tri_ref.shape[1]
    i = pl.program_id(2)
    nq = pl.num_programs(2) - 1
    lane = lax.broadcasted_iota(jnp.int32, (tq, ATTN_LANES), 1)
    tri = tri_ref[...]
    done = jnp.maximum(i - 1, 0)

    def stack_heads(q):
        q = q * (HEAD_DIM ** -0.5)
        zero = jnp.zeros_like(q)
        return jnp.concatenate(
            [jnp.where(lane // HEAD_DIM == h, q, zero) for h in range(HEADS_PER_BLOCK)], axis=0)

    def tile_rows(j):
        return pl.ds(pl.multiple_of(j * tk, tk), tk)

    def scores(q2, k_tiles):
        return lax.dot_general(q2, k_tiles, (((1,), (1,)), ((), ())), preferred_element_type=F32)

    def later_sums(z):
        return _bdot(_softplus(z).astype(BF16), tri)

    def first_scores(q2, j, lanes):
        prev = jnp.maximum(j - 1, 0)
        return scores(q2, jnp.concatenate(
            [k_ref[tile_rows(j), lanes], k_ref[tile_rows(prev), lanes]], axis=0))

    @pl.when(i == 0)
    def _():
        for lb in range(LANE_BLOCKS):
            lanes = slice(lb * ATTN_LANES, (lb + 1) * ATTN_LANES)
            z_scr[lb] = first_scores(stack_heads(qp_ref[:, lanes]), 0, lanes)
        att_scr[...] = jnp.zeros(att_scr.shape, BF16)
        carry_scr[...] = jnp.full(carry_scr.shape, 2.0 * EXP_UNDERFLOW, F32)

    def first_steps(lb):
        lanes = slice(lb * ATTN_LANES, (lb + 1) * ATTN_LANES)

        carry_done = carry_scr[lb]
        v_prev = v_ref[tile_rows(jnp.maximum(done - 1, 0)), lanes]
        v_both = jnp.concatenate(
            [v_ref[tile_rows(done), lanes], jnp.where(done > 0, v_prev, jnp.zeros_like(v_prev))],
            axis=0)
        acc = _bdot(att_scr[lb], v_both)

        z = z_scr[lb]
        z_scr[lb] = first_scores(stack_heads(qn_ref[:, lanes]), jnp.minimum(i + 1, nq - 1),
                                 lanes)
        rq = lax.broadcasted_iota(jnp.int32, (HEADS_PER_BLOCK * tq, tk), 0) % tq
        ck = lax.broadcasted_iota(jnp.int32, (HEADS_PER_BLOCK * tq, tk), 1)
        z_diag = jnp.where(ck < rq, z[:, :tk], MASKED_LOGIT)
        z_prev = z[:, tk:]
        within = later_sums(jnp.concatenate([z_diag, z_prev], axis=0))
        within_diag = within[:HEADS_PER_BLOCK * tq]
        within_prev = within[HEADS_PER_BLOCK * tq:]
        carry = within_diag[:, 0:1]
        att_scr[lb] = jnp.concatenate(
            [jnp.exp(z_diag - within_diag), jnp.exp(z_prev - (within_prev + carry))],
            axis=1).astype(BF16)
        carry_scr[lb] = carry + within_prev[:, 0:1]
        return acc, carry_done

    def earlier_tiles(lb, acc, carry_done):
        lanes = slice(lb * ATTN_LANES, (lb + 1) * ATTN_LANES)

        def cond(state):
            j, low, _, _ = state
            return jnp.logical_and(j >= 0, low < EXP_UNDERFLOW)

        def body(state):
            j, _, acc, carry = state
            z = scores(stack_heads(qp_ref[:, lanes]), k_ref[tile_rows(j), lanes])
            within = later_sums(z)
            att = jnp.exp(z - (within + carry))
            acc = acc + _bdot(att.astype(BF16), v_ref[tile_rows(j), lanes])
            carry = carry + within[:, 0:1]
            return j - 1, jnp.min(carry), acc, carry

        _, _, acc, _ = lax.while_loop(cond, body, (i - 3, jnp.min(carry_done), acc, carry_done))
        out = acc[:tq]
        for h in range(1, HEADS_PER_BLOCK):
            out = jnp.where(lane // HEAD_DIM == h, acc[h * tq:(h + 1) * tq], out)
        o_ref[:, lanes] = out.astype(o_ref.dtype)

    started = [first_steps(lb) for lb in range(LANE_BLOCKS)]
    for lb, (acc, carry_done) in enumerate(started):
        earlier_tiles(lb, acc, carry_done)


def _attn(q, k, v, batch, tq):
    t, width = q.shape
    seq = t // batch
    nq = seq // tq
    tk = tq
    group = LANE_BLOCKS * ATTN_LANES
    later = jnp.arange(tk)[:, None] >= jnp.arange(tk)[None, :]

    def q_tile(shift):
        return pl.BlockSpec(
            (tq, group), lambda b, hp, i: (b * nq + jnp.clip(i + shift, 0, nq - 1), hp))

    kv_spec = pl.BlockSpec((seq, group), lambda b, hp, i: (b, hp))
    rows = HEADS_PER_BLOCK * tq
    return pl.pallas_call(
        _attn_kernel,
        grid=(batch, width // group, nq + 1),
        in_specs=[q_tile(-1), q_tile(1), kv_spec, kv_spec, _resident((tk, tk))],
        out_specs=q_tile(-1),
        out_shape=jax.ShapeDtypeStruct((t, width), BF16),
        scratch_shapes=[pltpu.VMEM((LANE_BLOCKS, rows, 2 * tk), F32),
                        pltpu.VMEM((LANE_BLOCKS, rows, 2 * tk), BF16),
                        pltpu.VMEM((LANE_BLOCKS, rows, 1), F32)],
        compiler_params=pltpu.CompilerParams(
            dimension_semantics=("arbitrary", "arbitrary", "arbitrary"),
            vmem_limit_bytes=VMEM_LIMIT),
        name="attn",
    )(q, q, k, v, later.astype(BF16))


def _post_kernel(x_ref, yl_ref, ya_ref, gl_ref, ga_ref, p_ref, wbl_ref, wba_ref, wo_ref,
                 g_mlp_ref, wu_ref, wd_ref, g_ple_ref, wpg_ref, wpl_ref, g_fin_ref, o_ref):
    merged = (_sigmoid(gl_ref[...]) * _bdot(yl_ref[...], wbl_ref[...])
              + _sigmoid(ga_ref[...]) * _bdot(ya_ref[...], wba_ref[...]))
    x = x_ref[...] + _bdot(merged.astype(BF16), wo_ref[...])
    h2 = _rmsnorm(x, g_mlp_ref[...]).astype(BF16)
    up = jnp.square(jnp.maximum(_bdot(h2, wu_ref[...]), 0.0))
    x = x + _bdot(up.astype(BF16), wd_ref[...])
    h3 = _rmsnorm(x, g_ple_ref[...]).astype(BF16)
    gate = _sigmoid(_bdot(h3, wpg_ref[...]))
    x = x + gate * _bdot(p_ref[...].astype(BF16), wpl_ref[...])
    o_ref[...] = _rmsnorm(x, g_fin_ref[...])


def _post(x2, y_lru, y_att, g_lru, g_att, p2, w_bl, w_ba, w_o, g_mlp, w_up, w_down, g_ple, w_pg,
          w_pl, g_fin, tm):
    t, d = x2.shape
    tile = pl.BlockSpec((tm, d), lambda i: (i, 0))
    ptile = pl.BlockSpec((tm, p2.shape[1]), lambda i: (i, 0))
    row = _resident((1, d))
    square = _resident((d, d))
    return pl.pallas_call(
        _post_kernel,
        grid=(t // tm,),
        in_specs=[tile] * 5 + [ptile, square, square, square, row, _resident(w_up.shape),
                               _resident(w_down.shape), row, square, _resident(w_pl.shape), row],
        out_specs=tile,
        out_shape=jax.ShapeDtypeStruct((t, d), F32),
        compiler_params=pltpu.CompilerParams(
            dimension_semantics=("arbitrary",), vmem_limit_bytes=VMEM_LIMIT),
        name="post",
    )(x2, y_lru, y_att, g_lru, g_att, p2, w_bl, w_ba, w_o, g_mlp, w_up, w_down, g_ple, w_pg,
      w_pl, g_fin)


def kernel(x, p, norm_mix_g, w_in, conv_w, conv_b, w_rgate, b_rgate, w_igate, b_igate, lru_lambda, w_br_lru, w_br_att, w_out, norm_mlp_g, w_mlp_up, w_mlp_down, norm_ple_g, w_ple_gate, w_ple, norm_final_g):
    batch, seq, d = x.shape
    depth = w_in.shape[0]
    t = batch * seq
    x2 = x.reshape(t, d)
    for l in range(depth):
        q, k, v, g_lru, g_att, y_lru = _in_lru(
            x2, norm_mix_g[l].reshape(1, d), w_in[l].astype(BF16), conv_w[l], conv_b[l],
            w_rgate[l], b_rgate[l], w_igate[l], b_igate[l], lru_lambda[l], batch, ts=512)
        y_att = _attn(q, k, v, batch, tq=256)
        assert l == depth - 1, "deeper stacks need a post variant without the final norm"
        x2 = _post(x2, y_lru, y_att, g_lru, g_att, p[l].reshape(t, -1),
                   w_br_lru[l].astype(BF16), w_br_att[l].astype(BF16), w_out[l].astype(BF16),
                   norm_mlp_g[l].reshape(1, d), w_mlp_up[l].astype(BF16),
                   w_mlp_down[l].astype(BF16), norm_ple_g[l].reshape(1, d),
                   w_ple_gate[l].astype(BF16), w_ple[l].astype(BF16),
                   norm_final_g.reshape(1, d), tm=512)
    return x2.reshape(batch, seq, d)
```

```python
import jax
import jax.numpy as jnp
from jax import lax
from jax.experimental import pallas as pl
from jax.experimental.pallas import tpu as pltpu

F32 = jnp.float32
BF16 = jnp.bfloat16

NORM_EPS = 1e-6
LRU_C = 8.0
CONV_W = 4
HEAD_DIM = 64
SUBLANES = 8
MXU_DIM = 256
ATTN_LANES = 128
HEADS_PER_BLOCK = ATTN_LANES // HEAD_DIM

LOG2E = 1.4426950408889634
EXP_UNDERFLOW = 105.0
MASKED_LOGIT = -1e30

VMEM_LIMIT = 60 * 1024 * 1024


def _resident(shape):
    nd = len(shape)
    return pl.BlockSpec(shape, lambda *_: (0,) * nd, pipeline_mode=pl.Buffered(1))


def _rmsnorm(x, g):
    return x * lax.rsqrt(jnp.mean(x * x, axis=-1, keepdims=True) + NORM_EPS) * g


def _sigmoid(x):
    return 0.5 * jnp.tanh(0.5 * x) + 0.5


def _bdot(a, b):
    return jnp.dot(a, b, preferred_element_type=F32)


def _shift_rows(cur, prev_tail, back, row):
    rolled = pltpu.roll(cur, back, axis=0)
    first = jnp.where(row < back, pltpu.roll(prev_tail, back, axis=0), rolled[0:SUBLANES])
    return jnp.concatenate([first, rolled[SUBLANES:]], axis=0)


def _in_lru_kernel(x_ref, g_ref, w_ref, cw_ref, cb_ref, wr_ref, br_ref, wi_ref, bi_ref, lam_ref,
                   *rest):
    n_cast = (len(rest) - 11) // 2
    cast_srcs, rest = rest[:n_cast], rest[n_cast:]
    q_ref, k_ref, v_ref, gl_ref, ga_ref, y_ref = rest[:6]
    cast_dsts, (tail, a_scr, u_scr, h_scr, hprev) = rest[6:6 + n_cast], rest[6 + n_cast:]
    ts, d = x_ref.shape
    for src, dst in zip(cast_srcs, cast_dsts):
        dst[...] = src[...].astype(dst.dtype)

    @pl.when(pl.program_id(1) == 0)
    def _():
        tail[...] = jnp.zeros(tail.shape, F32)
        hprev[...] = jnp.zeros(hprev.shape, F32)

    h = _rmsnorm(x_ref[...], g_ref[...]).astype(BF16)

    def proj(n):
        return _bdot(h, w_ref[:, n * d:(n + 1) * d])

    outs = {2: q_ref, 3: k_ref, 4: v_ref, 5: gl_ref, 6: ga_ref}

    def emit(n):
        outs[n][...] = proj(n).astype(outs[n].dtype)

    ux = proj(0)
    ug = proj(1)
    row = lax.broadcasted_iota(jnp.int32, (SUBLANES, d), 0)
    prev_tail = tail[...]
    c = cb_ref[...]
    for k in range(CONV_W):
        back = CONV_W - 1 - k
        xs = ux if back == 0 else _shift_rows(ux, prev_tail, back, row)
        c = c + cw_ref[k:k + 1, :] * xs
    tail[...] = ux[ts - SUBLANES:, :]
    emit(2)

    cb = c.astype(BF16)
    gr, gi = [], []
    for j in range(d // MXU_DIM):
        sl = slice(j * MXU_DIM, (j + 1) * MXU_DIM)
        gr.append(_bdot(cb[:, sl], wr_ref[j]))
        gi.append(_bdot(cb[:, sl], wi_ref[j]))
    r = _sigmoid(jnp.concatenate(gr, axis=1) + br_ref[...])
    ig = _sigmoid(jnp.concatenate(gi, axis=1) + bi_ref[...])
    emit(3)

    neg_lam = -lam_ref[...]
    softplus = jnp.maximum(neg_lam, 0.0) + jnp.log1p(jnp.exp(-jnp.abs(neg_lam)))
    log_a = (-LRU_C) * r * softplus
    a = jnp.exp(log_a)
    one_m_a2 = jnp.tanh(-log_a) * (a * a + 1.0)
    mult = jnp.where(one_m_a2 > 0.0, one_m_a2 * lax.rsqrt(one_m_a2), 0.0)
    a_scr[...] = a
    u_scr[...] = mult * (ig * c)
    emit(4)

    h_in = hprev[...]
    for g0 in range(0, ts, SUBLANES):
        if g0 == ts // 2:
            emit(5)
        a8 = a_scr[g0:g0 + SUBLANES, :]
        u8 = u_scr[g0:g0 + SUBLANES, :]
        for dist in (1, 2, 4):
            keep = row >= dist
            u_sh = pltpu.roll(u8, dist, axis=0)
            a_sh = pltpu.roll(a8, dist, axis=0)
            u8 = jnp.where(keep, a8 * u_sh + u8, u8)
            a8 = jnp.where(keep, a8 * a_sh, a8)
        h8 = a8 * h_in + u8
        h_scr[g0:g0 + SUBLANES, :] = h8
        h_in = h8[SUBLANES - 1:SUBLANES, :]
    hprev[...] = h_in
    emit(6)

    y_ref[...] = (h_scr[...] * jax.nn.gelu(ug, approximate=True)).astype(y_ref.dtype)


def _block_diag(wb, group):
    n, c, _ = wb.shape
    wg = wb.reshape(n // group, group, c, c)
    eye = jnp.eye(group, dtype=wb.dtype)
    dense = wg[:, :, :, None, :] * eye[None, :, None, :, None]
    return dense.reshape(n // group, group * c, group * c)


def _in_lru(x2, g, w_bf, conv_w, conv_b, w_r, b_r, w_i, b_i, lam, later_weights, batch, ts):
    t, d = x2.shape
    nt = t // batch // ts
    steps = batch * nt
    slabs = [pl.BlockSpec((w.shape[0] // steps, w.shape[1]), lambda b, i: (b * nt + i, 0))
             for w in later_weights]
    group = MXU_DIM // w_r.shape[1]
    wr = _block_diag(w_r, group).astype(BF16)
    wi = _block_diag(w_i, group).astype(BF16)
    tile = pl.BlockSpec((ts, d), lambda b, i: (b * nt + i, 0))
    row = _resident((1, d))
    dtypes = [BF16, BF16, BF16, F32, F32, BF16]
    outs = pl.pallas_call(
        _in_lru_kernel,
        grid=(batch, nt),
        in_specs=[tile, row, _resident(w_bf.shape), _resident(conv_w.shape), row,
                  _resident(wr.shape), row, _resident(wi.shape), row, row] + slabs,
        out_specs=[tile] * len(dtypes) + slabs,
        out_shape=([jax.ShapeDtypeStruct((t, d), dt) for dt in dtypes]
                   + [jax.ShapeDtypeStruct(w.shape, BF16) for w in later_weights]),
        scratch_shapes=[pltpu.VMEM((SUBLANES, d), F32), pltpu.VMEM((ts, d), F32),
                        pltpu.VMEM((ts, d), F32), pltpu.VMEM((ts, d), F32),
                        pltpu.VMEM((1, d), F32)],
        compiler_params=pltpu.CompilerParams(
            dimension_semantics=("arbitrary", "arbitrary"), vmem_limit_bytes=VMEM_LIMIT),
        name="in_lru",
    )(x2, g, w_bf, conv_w, conv_b.reshape(1, d), wr, b_r.reshape(1, d), wi, b_i.reshape(1, d),
      lam.reshape(1, d), *later_weights)
    return outs[:len(dtypes)], outs[len(dtypes):]


def _softplus(z):
    return jnp.maximum(z, 0.0) + jnp.log(1.0 + jnp.exp2(jnp.abs(z) * (-LOG2E)))


LANE_BLOCKS = 4


def _attn_kernel(qp_ref, qn_ref, k_ref, v_ref, tri_ref, o_ref, z_scr, att_scr, carry_scr):
    tq = qp_ref.shape[0]
    tk = tri_ref.shape[1]
    i = pl.program_id(2)
    nq = pl.num_programs(2) - 1
    lane = lax.broadcasted_iota(jnp.int32, (tq, ATTN_LANES), 1)
    tri = tri_ref[...]
    done = jnp.maximum(i - 1, 0)

    def stack_heads(q):
        q = q * (HEAD_DIM ** -0.5)
        zero = jnp.zeros_like(q)
        return jnp.concatenate(
            [jnp.where(lane // HEAD_DIM == h, q, zero) for h in range(HEADS_PER_BLOCK)], axis=0)

    def tile_rows(j):
        return pl.ds(pl.multiple_of(j * tk, tk), tk)

    def scores(q2, k_tiles):
        return lax.dot_general(q2, k_tiles, (((1,), (1,)), ((), ())), preferred_element_type=F32)

    def later_sums(z):
        return _bdot(_softplus(z).astype(BF16), tri)

    def first_scores(q2, j, lanes):
        prev = jnp.maximum(j - 1, 0)
        return scores(q2, jnp.concatenate(
            [k_ref[tile_rows(j), lanes], k_ref[tile_rows(prev), lanes]], axis=0))

    @pl.when(i == 0)
    def _():
        for lb in range(LANE_BLOCKS):
            lanes = slice(lb * ATTN_LANES, (lb + 1) * ATTN_LANES)
            z_scr[lb] = first_scores(stack_heads(qp_ref[:, lanes]), 0, lanes)
        att_scr[...] = jnp.zeros(att_scr.shape, BF16)
        carry_scr[...] = jnp.full(carry_scr.shape, 2.0 * EXP_UNDERFLOW, F32)

    def first_steps(lb):
        lanes = slice(lb * ATTN_LANES, (lb + 1) * ATTN_LANES)

        carry_done = carry_scr[lb]
        v_prev = v_ref[tile_rows(jnp.maximum(done - 1, 0)), lanes]
        v_both = jnp.concatenate(
            [v_ref[tile_rows(done), lanes], jnp.where(done > 0, v_prev, jnp.zeros_like(v_prev))],
            axis=0)
        acc = _bdot(att_scr[lb], v_both)

        z = z_scr[lb]
        z_scr[lb] = first_scores(stack_heads(qn_ref[:, lanes]), jnp.minimum(i + 1, nq - 1),
                                 lanes)
        rq = lax.broadcasted_iota(jnp.int32, (HEADS_PER_BLOCK * tq, tk), 0) % tq
        ck = lax.broadcasted_iota(jnp.int32, (HEADS_PER_BLOCK * tq, tk), 1)
        z_diag = jnp.where(ck < rq, z[:, :tk], MASKED_LOGIT)
        z_prev = z[:, tk:]
        within = later_sums(jnp.concatenate([z_diag, z_prev], axis=0))
        within_diag = within[:HEADS_PER_BLOCK * tq]
        within_prev = within[HEADS_PER_BLOCK * tq:]
        carry = within_diag[:, 0:1]
        att_scr[lb] = jnp.concatenate(
            [jnp.exp(z_diag - within_diag), jnp.exp(z_prev - (within_prev + carry))],
            axis=1).astype(BF16)
        carry_scr[lb] = carry + within_prev[:, 0:1]
        return acc, carry_done

    def earlier_tiles(lb, acc, carry_done):
        lanes = slice(lb * ATTN_LANES, (lb + 1) * ATTN_LANES)

        def cond(state):
            j, low, _, _ = state
            return jnp.logical_and(j >= 0, low < EXP_UNDERFLOW)

        def body(state):
            j, _, acc, carry = state
            z = scores(stack_heads(qp_ref[:, lanes]), k_ref[tile_rows(j), lanes])
            within = later_sums(z)
            att = jnp.exp(z - (within + carry))
            acc = acc + _bdot(att.astype(BF16), v_ref[tile_rows(j), lanes])
            carry = carry + within[:, 0:1]
            return j - 1, jnp.min(carry), acc, carry

        _, _, acc, _ = lax.while_loop(cond, body, (i - 3, jnp.min(carry_done), acc, carry_done))
        out = acc[:tq]
        for h in range(1, HEADS_PER_BLOCK):
            out = jnp.where(lane // HEAD_DIM == h, acc[h * tq:(h + 1) * tq], out)
        o_ref[:, lanes] = out.astype(o_ref.dtype)

    started = [first_steps(lb) for lb in range(LANE_BLOCKS)]
    for lb, (acc, carry_done) in enumerate(started):
        earlier_tiles(lb, acc, carry_done)


def _attn(q, k, v, batch, tq):
    t, width = q.shape
    seq = t // batch
    nq = seq // tq
    tk = tq
    group = LANE_BLOCKS * ATTN_LANES
    later = jnp.arange(tk)[:, None] >= jnp.arange(tk)[None, :]

    def q_tile(shift):
        return pl.BlockSpec(
            (tq, group), lambda b, hp, i: (b * nq + jnp.clip(i + shift, 0, nq - 1), hp))

    kv_spec = pl.BlockSpec((seq, group), lambda b, hp, i: (b, hp))
    rows = HEADS_PER_BLOCK * tq
    return pl.pallas_call(
        _attn_kernel,
        grid=(batch, width // group, nq + 1),
        in_specs=[q_tile(-1), q_tile(1), kv_spec, kv_spec, _resident((tk, tk))],
        out_specs=q_tile(-1),
        out_shape=jax.ShapeDtypeStruct((t, width), BF16),
        scratch_shapes=[pltpu.VMEM((LANE_BLOCKS, rows, 2 * tk), F32),
                        pltpu.VMEM((LANE_BLOCKS, rows, 2 * tk), BF16),
                        pltpu.VMEM((LANE_BLOCKS, rows, 1), F32)],
        compiler_params=pltpu.CompilerParams(
            dimension_semantics=("arbitrary", "arbitrary", "arbitrary"),
            vmem_limit_bytes=VMEM_LIMIT),
        name="attn",
    )(q, q, k, v, later.astype(BF16))


def _post_kernel(x_ref, yl_ref, ya_ref, gl_ref, ga_ref, p_ref, wbl_ref, wba_ref, wo_ref,
                 g_mlp_ref, wu_ref, wd_ref, g_ple_ref, wpg_ref, wpl_ref, g_fin_ref, o_ref):
    merged = (_sigmoid(gl_ref[...]) * _bdot(yl_ref[...], wbl_ref[...])
              + _sigmoid(ga_ref[...]) * _bdot(ya_ref[...], wba_ref[...]))
    x = x_ref[...] + _bdot(merged.astype(BF16), wo_ref[...])
    h2 = _rmsnorm(x, g_mlp_ref[...]).astype(BF16)
    up = jnp.square(jnp.maximum(_bdot(h2, wu_ref[...]), 0.0))
    x = x + _bdot(up.astype(BF16), wd_ref[...])
    h3 = _rmsnorm(x, g_ple_ref[...]).astype(BF16)
    gate = _sigmoid(_bdot(h3, wpg_ref[...]))
    x = x + gate * _bdot(p_ref[...].astype(BF16), wpl_ref[...])
    o_ref[...] = _rmsnorm(x, g_fin_ref[...])


def _post(x2, y_lru, y_att, g_lru, g_att, p2, w_bl, w_ba, w_o, g_mlp, w_up, w_down, g_ple, w_pg,
          w_pl, g_fin, tm):
    t, d = x2.shape
    tile = pl.BlockSpec((tm, d), lambda i: (i, 0))
    ptile = pl.BlockSpec((tm, p2.shape[1]), lambda i: (i, 0))
    row = _resident((1, d))
    square = _resident((d, d))
    return pl.pallas_call(
        _post_kernel,
        grid=(t // tm,),
        in_specs=[tile] * 5 + [ptile, square, square, square, row, _resident(w_up.shape),
                               _resident(w_down.shape), row, square, _resident(w_pl.shape), row],
        out_specs=tile,
        out_shape=jax.ShapeDtypeStruct((t, d), F32),
        compiler_params=pltpu.CompilerParams(
            dimension_semantics=("arbitrary",), vmem_limit_bytes=VMEM_LIMIT),
        name="post",
    )(x2, y_lru, y_att, g_lru, g_att, p2, w_bl, w_ba, w_o, g_mlp, w_up, w_down, g_ple, w_pg,
      w_pl, g_fin)


def kernel(x, p, norm_mix_g, w_in, conv_w, conv_b, w_rgate, b_rgate, w_igate, b_igate, lru_lambda, w_br_lru, w_br_att, w_out, norm_mlp_g, w_mlp_up, w_mlp_down, norm_ple_g, w_ple_gate, w_ple, norm_final_g):
    batch, seq, d = x.shape
    depth = w_in.shape[0]
    t = batch * seq
    x2 = x.reshape(t, d)
    for l in range(depth):
        later = [w_br_lru[l], w_br_att[l], w_out[l], w_mlp_up[l], w_mlp_down[l], w_ple_gate[l]]
        (q, k, v, g_lru, g_att, y_lru), later_bf = _in_lru(
            x2, norm_mix_g[l].reshape(1, d), w_in[l].astype(BF16), conv_w[l], conv_b[l],
            w_rgate[l], b_rgate[l], w_igate[l], b_igate[l], lru_lambda[l], later, batch, ts=512)
        w_bl, w_ba, w_o, w_up, w_down, w_pg = later_bf
        y_att = _attn(q, k, v, batch, tq=256)
        assert l == depth - 1, "deeper stacks need a post variant without the final norm"
        x2 = _post(x2, y_lru, y_att, g_lru, g_att, p[l].reshape(t, -1), w_bl, w_ba, w_o,
                   norm_mlp_g[l].reshape(1, d), w_up, w_down, norm_ple_g[l].reshape(1, d),
                   w_pg, w_ple[l].astype(BF16), norm_final_g.reshape(1, d), tm=512)
    return x2.reshape(batch, seq, d)
```

```python
import jax
import jax.numpy as jnp
from jax import lax
from jax.experimental import pallas as pl
from jax.experimental.pallas import tpu as pltpu

F32 = jnp.float32
BF16 = jnp.bfloat16

NORM_EPS = 1e-6
LRU_C = 8.0
CONV_W = 4
HEAD_DIM = 64
SUBLANES = 8
MXU_DIM = 256
ATTN_LANES = 128
HEADS_PER_BLOCK = ATTN_LANES // HEAD_DIM

LOG2E = 1.4426950408889634
EXP_UNDERFLOW = 105.0
MASKED_LOGIT = -1e30

VMEM_LIMIT = 60 * 1024 * 1024


def _resident(shape):
    nd = len(shape)
    return pl.BlockSpec(shape, lambda *_: (0,) * nd, pipeline_mode=pl.Buffered(1))


def _rmsnorm(x, g):
    return x * lax.rsqrt(jnp.mean(x * x, axis=-1, keepdims=True) + NORM_EPS) * g


def _sigmoid(x):
    return 0.5 * jnp.tanh(0.5 * x) + 0.5


def _bdot(a, b):
    return jnp.dot(a, b, preferred_element_type=F32)


def _shift_rows(cur, prev_tail, back, row):
    rolled = pltpu.roll(cur, back, axis=0)
    first = jnp.where(row < back, pltpu.roll(prev_tail, back, axis=0), rolled[0:SUBLANES])
    return jnp.concatenate([first, rolled[SUBLANES:]], axis=0)


def _in_lru_kernel(x_ref, g_ref, w_ref, cw_ref, cb_ref, wr_ref, br_ref, wi_ref, bi_ref, lam_ref,
                   *rest):
    n_cast = (len(rest) - 11) // 2
    cast_srcs, rest = rest[:n_cast], rest[n_cast:]
    q_ref, k_ref, v_ref, gl_ref, ga_ref, y_ref = rest[:6]
    cast_dsts, (tail, a_scr, u_scr, h_scr, hprev) = rest[6:6 + n_cast], rest[6 + n_cast:]
    ts, d = x_ref.shape
    for src, dst in zip(cast_srcs, cast_dsts):
        dst[...] = src[...].astype(dst.dtype)

    @pl.when(pl.program_id(1) == 0)
    def _():
        tail[...] = jnp.zeros(tail.shape, F32)
        hprev[...] = jnp.zeros(hprev.shape, F32)

    h = _rmsnorm(x_ref[...], g_ref[...]).astype(BF16)

    def proj(n):
        return _bdot(h, w_ref[:, n * d:(n + 1) * d])

    outs = {2: q_ref, 3: k_ref, 4: v_ref, 5: gl_ref, 6: ga_ref}

    def emit(n):
        outs[n][...] = proj(n).astype(outs[n].dtype)

    ux = proj(0)
    ug = proj(1)
    row = lax.broadcasted_iota(jnp.int32, (SUBLANES, d), 0)
    prev_tail = tail[...]
    c = cb_ref[...]
    for k in range(CONV_W):
        back = CONV_W - 1 - k
        xs = ux if back == 0 else _shift_rows(ux, prev_tail, back, row)
        c = c + cw_ref[k:k + 1, :] * xs
    tail[...] = ux[ts - SUBLANES:, :]
    emit(2)

    cb = c.astype(BF16)
    gr, gi = [], []
    for j in range(d // MXU_DIM):
        sl = slice(j * MXU_DIM, (j + 1) * MXU_DIM)
        gr.append(_bdot(cb[:, sl], wr_ref[j]))
        gi.append(_bdot(cb[:, sl], wi_ref[j]))
    r = _sigmoid(jnp.concatenate(gr, axis=1) + br_ref[...])
    ig = _sigmoid(jnp.concatenate(gi, axis=1) + bi_ref[...])
    emit(3)

    neg_lam = -lam_ref[...]
    softplus = jnp.maximum(neg_lam, 0.0) + jnp.log1p(jnp.exp(-jnp.abs(neg_lam)))
    log_a = (-LRU_C) * r * softplus
    a = jnp.exp(log_a)
    one_m_a2 = jnp.tanh(-log_a) * (a * a + 1.0)
    mult = jnp.where(one_m_a2 > 0.0, one_m_a2 * lax.rsqrt(one_m_a2), 0.0)
    a_scr[...] = a
    u_scr[...] = mult * (ig * c)
    emit(4)

    h_in = hprev[...]
    for g0 in range(0, ts, SUBLANES):
        if g0 == ts // 2:
            emit(5)
        a8 = a_scr[g0:g0 + SUBLANES, :]
        u8 = u_scr[g0:g0 + SUBLANES, :]
        for dist in (1, 2, 4):
            keep = row >= dist
            u_sh = pltpu.roll(u8, dist, axis=0)
            a_sh = pltpu.roll(a8, dist, axis=0)
            u8 = jnp.where(keep, a8 * u_sh + u8, u8)
            a8 = jnp.where(keep, a8 * a_sh, a8)
        h8 = a8 * h_in + u8
        h_scr[g0:g0 + SUBLANES, :] = h8
        h_in = h8[SUBLANES - 1:SUBLANES, :]
    hprev[...] = h_in
    emit(6)

    y_ref[...] = (h_scr[...] * jax.nn.gelu(ug, approximate=True)).astype(y_ref.dtype)


def _block_diag(wb, group):
    n, c, _ = wb.shape
    wg = wb.reshape(n // group, group, c, c)
    eye = jnp.eye(group, dtype=wb.dtype)
    dense = wg[:, :, :, None, :] * eye[None, :, None, :, None]
    return dense.reshape(n // group, group * c, group * c)


def _in_lru(x2, g, w_bf, conv_w, conv_b, w_r, b_r, w_i, b_i, lam, later_weights, batch, ts):
    t, d = x2.shape
    nt = t // batch // ts
    steps = batch * nt
    slabs = [pl.BlockSpec((w.shape[0] // steps, w.shape[1]), lambda b, i: (b * nt + i, 0))
             for w in later_weights]
    group = MXU_DIM // w_r.shape[1]
    wr = _block_diag(w_r, group).astype(BF16)
    wi = _block_diag(w_i, group).astype(BF16)
    tile = pl.BlockSpec((ts, d), lambda b, i: (b * nt + i, 0))
    row = _resident((1, d))
    dtypes = [BF16, BF16, BF16, F32, F32, BF16]
    outs = pl.pallas_call(
        _in_lru_kernel,
        grid=(batch, nt),
        in_specs=[tile, row, _resident(w_bf.shape), _resident(conv_w.shape), row,
                  _resident(wr.shape), row, _resident(wi.shape), row, row] + slabs,
        out_specs=[tile] * len(dtypes) + slabs,
        out_shape=([jax.ShapeDtypeStruct((t, d), dt) for dt in dtypes]
                   + [jax.ShapeDtypeStruct(w.shape, BF16) for w in later_weights]),
        scratch_shapes=[pltpu.VMEM((SUBLANES, d), F32), pltpu.VMEM((ts, d), F32),
                        pltpu.VMEM((ts, d), F32), pltpu.VMEM((ts, d), F32),
                        pltpu.VMEM((1, d), F32)],
        compiler_params=pltpu.CompilerParams(
            dimension_semantics=("arbitrary", "arbitrary"), vmem_limit_bytes=VMEM_LIMIT),
        name="in_lru",
    )(x2, g, w_bf, conv_w, conv_b.reshape(1, d), wr, b_r.reshape(1, d), wi, b_i.reshape(1, d),
      lam.reshape(1, d), *later_weights)
    return outs[:len(dtypes)], outs[len(dtypes):]


def _softplus(z):
    return jnp.maximum(z, 0.0) + jnp.log(1.0 + jnp.exp2(jnp.abs(z) * (-LOG2E)))


LANE_BLOCKS = 4


def _attn_kernel(qp_ref, qn_ref, k_ref, v_ref, tri_ref, o_ref, z_scr, att_scr, carry_scr):
    tq = qp_ref.shape[0]
    tk = tri_ref.shape[1]
    i = pl.program_id(2)
    nq = pl.num_programs(2) - 1
    lane = lax.broadcasted_iota(jnp.int32, (tq, ATTN_LANES), 1)
    tri = tri_ref[...]
    done = jnp.maximum(i - 1, 0)

    def stack_heads(q):
        q = q * (HEAD_DIM ** -0.5)
        zero = jnp.zeros_like(q)
        return jnp.concatenate(
            [jnp.where(lane // HEAD_DIM == h, q, zero) for h in range(HEADS_PER_BLOCK)], axis=0)

    def tile_rows(j):
        return pl.ds(pl.multiple_of(j * tk, tk), tk)

    def scores(q2, k_tiles):
        return lax.dot_general(q2, k_tiles, (((1,), (1,)), ((), ())), preferred_element_type=F32)

    def decay_terms(z):
        pos = _softplus(z)
        pos_bf = pos.astype(BF16)
        later = _bdot(pos_bf, tri)
        return z - pos, later, later[:, 0:1] + pos_bf[:, 0:1].astype(F32)

    def first_scores(q2, j, lanes):
        prev = jnp.maximum(j - 1, 0)
        return scores(q2, jnp.concatenate(
            [k_ref[tile_rows(j), lanes], k_ref[tile_rows(prev), lanes]], axis=0))

    @pl.when(i == 0)
    def _():
        for lb in range(LANE_BLOCKS):
            lanes = slice(lb * ATTN_LANES, (lb + 1) * ATTN_LANES)
            z_scr[lb] = first_scores(stack_heads(qp_ref[:, lanes]), 0, lanes)
        att_scr[...] = jnp.zeros(att_scr.shape, BF16)
        carry_scr[...] = jnp.full(carry_scr.shape, 2.0 * EXP_UNDERFLOW, F32)

    def first_steps(lb):
        lanes = slice(lb * ATTN_LANES, (lb + 1) * ATTN_LANES)

        carry_done = carry_scr[lb]
        v_prev = v_ref[tile_rows(jnp.maximum(done - 1, 0)), lanes]
        v_both = jnp.concatenate(
            [v_ref[tile_rows(done), lanes], jnp.where(done > 0, v_prev, jnp.zeros_like(v_prev))],
            axis=0)
        acc = _bdot(att_scr[lb], v_both)

        z = z_scr[lb]
        z_scr[lb] = first_scores(stack_heads(qn_ref[:, lanes]), jnp.minimum(i + 1, nq - 1),
                                 lanes)
        rq = lax.broadcasted_iota(jnp.int32, (HEADS_PER_BLOCK * tq, tk), 0) % tq
        ck = lax.broadcasted_iota(jnp.int32, (HEADS_PER_BLOCK * tq, tk), 1)
        z_diag = jnp.where(ck < rq, z[:, :tk], MASKED_LOGIT)
        z_prev = z[:, tk:]
        rows = HEADS_PER_BLOCK * tq
        log_beta, later, total = decay_terms(jnp.concatenate([z_diag, z_prev], axis=0))
        carry = total[:rows]
        att_scr[lb] = jnp.concatenate(
            [jnp.exp(log_beta[:rows] - later[:rows]),
             jnp.exp(log_beta[rows:] - (later[rows:] + carry))], axis=1).astype(BF16)
        carry_scr[lb] = carry + total[rows:]
        return acc, carry_done

    def earlier_tiles(lb, acc, carry_done):
        lanes = slice(lb * ATTN_LANES, (lb + 1) * ATTN_LANES)

        def cond(state):
            j, low, _, _ = state
            return jnp.logical_and(j >= 0, low < EXP_UNDERFLOW)

        def body(state):
            j, _, acc, carry = state
            z = scores(stack_heads(qp_ref[:, lanes]), k_ref[tile_rows(j), lanes])
            log_beta, later, total = decay_terms(z)
            att = jnp.exp(log_beta - (later + carry))
            acc = acc + _bdot(att.astype(BF16), v_ref[tile_rows(j), lanes])
            carry = carry + total
            return j - 1, jnp.min(carry), acc, carry

        _, _, acc, _ = lax.while_loop(cond, body, (i - 3, jnp.min(carry_done), acc, carry_done))
        out = acc[:tq]
        for h in range(1, HEADS_PER_BLOCK):
            out = jnp.where(lane // HEAD_DIM == h, acc[h * tq:(h + 1) * tq], out)
        o_ref[:, lanes] = out.astype(o_ref.dtype)

    started = [first_steps(lb) for lb in range(LANE_BLOCKS)]
    for lb, (acc, carry_done) in enumerate(started):
        earlier_tiles(lb, acc, carry_done)


def _attn(q, k, v, batch, tq):
    t, width = q.shape
    seq = t // batch
    nq = seq // tq
    tk = tq
    group = LANE_BLOCKS * ATTN_LANES
    later = jnp.arange(tk)[:, None] > jnp.arange(tk)[None, :]

    def q_tile(shift):
        return pl.BlockSpec(
            (tq, group), lambda b, hp, i: (b * nq + jnp.clip(i + shift, 0, nq - 1), hp))

    kv_spec = pl.BlockSpec((seq, group), lambda b, hp, i: (b, hp))
    rows = HEADS_PER_BLOCK * tq
    return pl.pallas_call(
        _attn_kernel,
        grid=(batch, width // group, nq + 1),
        in_specs=[q_tile(-1), q_tile(1), kv_spec, kv_spec, _resident((tk, tk))],
        out_specs=q_tile(-1),
        out_shape=jax.ShapeDtypeStruct((t, width), BF16),
        scratch_shapes=[pltpu.VMEM((LANE_BLOCKS, rows, 2 * tk), F32),
                        pltpu.VMEM((LANE_BLOCKS, rows, 2 * tk), BF16),
                        pltpu.VMEM((LANE_BLOCKS, rows, 1), F32)],
        compiler_params=pltpu.CompilerParams(
            dimension_semantics=("arbitrary", "arbitrary", "arbitrary"),
            vmem_limit_bytes=VMEM_LIMIT),
        name="attn",
    )(q, q, k, v, later.astype(BF16))


def _post_kernel(x_ref, yl_ref, ya_ref, gl_ref, ga_ref, p_ref, wbl_ref, wba_ref, wo_ref,
                 g_mlp_ref, wu_ref, wd_ref, g_ple_ref, wpg_ref, wpl_ref, g_fin_ref, o_ref):
    merged = (_sigmoid(gl_ref[...]) * _bdot(yl_ref[...], wbl_ref[...])
              + _sigmoid(ga_ref[...]) * _bdot(ya_ref[...], wba_ref[...]))
    x = x_ref[...] + _bdot(merged.astype(BF16), wo_ref[...])
    h2 = _rmsnorm(x, g_mlp_ref[...]).astype(BF16)
    up = jnp.square(jnp.maximum(_bdot(h2, wu_ref[...]), 0.0))
    x = x + _bdot(up.astype(BF16), wd_ref[...])
    h3 = _rmsnorm(x, g_ple_ref[...]).astype(BF16)
    gate = _sigmoid(_bdot(h3, wpg_ref[...]))
    x = x + gate * _bdot(p_ref[...].astype(BF16), wpl_ref[...])
    o_ref[...] = _rmsnorm(x, g_fin_ref[...])


def _post(x2, y_lru, y_att, g_lru, g_att, p2, w_bl, w_ba, w_o, g_mlp, w_up, w_down, g_ple, w_pg,
          w_pl, g_fin, tm):
    t, d = x2.shape
    tile = pl.BlockSpec((tm, d), lambda i: (i, 0))
    ptile = pl.BlockSpec((tm, p2.shape[1]), lambda i: (i, 0))
    row = _resident((1, d))
    square = _resident((d, d))
    return pl.pallas_call(
        _post_kernel,
        grid=(t // tm,),
        in_specs=[tile] * 5 + [ptile, square, square, square, row, _resident(w_up.shape),
                               _resident(w_down.shape), row, square, _resident(w_pl.shape), row],
        out_specs=tile,
        out_shape=jax.ShapeDtypeStruct((t, d), F32),
        compiler_params=pltpu.CompilerParams(
            dimension_semantics=("arbitrary",), vmem_limit_bytes=VMEM_LIMIT),
        name="post",
    )(x2, y_lru, y_att, g_lru, g_att, p2, w_bl, w_ba, w_o, g_mlp, w_up, w_down, g_ple, w_pg,
      w_pl, g_fin)


def kernel(x, p, norm_mix_g, w_in, conv_w, conv_b, w_rgate, b_rgate, w_igate, b_igate, lru_lambda, w_br_lru, w_br_att, w_out, norm_mlp_g, w_mlp_up, w_mlp_down, norm_ple_g, w_ple_gate, w_ple, norm_final_g):
    batch, seq, d = x.shape
    depth = w_in.shape[0]
    t = batch * seq
    x2 = x.reshape(t, d)
    for l in range(depth):
        later = [w_br_lru[l], w_br_att[l], w_out[l], w_mlp_up[l], w_mlp_down[l], w_ple_gate[l]]
        (q, k, v, g_lru, g_att, y_lru), later_bf = _in_lru(
            x2, norm_mix_g[l].reshape(1, d), w_in[l].astype(BF16), conv_w[l], conv_b[l],
            w_rgate[l], b_rgate[l], w_igate[l], b_igate[l], lru_lambda[l], later, batch, ts=512)
        w_bl, w_ba, w_o, w_up, w_down, w_pg = later_bf
        y_att = _attn(q, k, v, batch, tq=256)
        assert l == depth - 1, "deeper stacks need a post variant without the final norm"
        x2 = _post(x2, y_lru, y_att, g_lru, g_att, p[l].reshape(t, -1), w_bl, w_ba, w_o,
                   norm_mlp_g[l].reshape(1, d), w_up, w_down, norm_ple_g[l].reshape(1, d),
                   w_pg, w_ple[l].astype(BF16), norm_final_g.reshape(1, d), tm=512)
    return x2.reshape(batch, seq, d)
```

```python
import jax
import jax.numpy as jnp
from jax import lax
from jax.experimental import pallas as pl
from jax.experimental.pallas import tpu as pltpu

F32 = jnp.float32
BF16 = jnp.bfloat16

NORM_EPS = 1e-6
LRU_C = 8.0
CONV_W = 4
HEAD_DIM = 64
SUBLANES = 8
MXU_DIM = 256
ATTN_LANES = 128
HEADS_PER_BLOCK = ATTN_LANES // HEAD_DIM

LOG2E = 1.4426950408889634
EXP_UNDERFLOW = 105.0
MASKED_LOGIT = -1e30

VMEM_LIMIT = 60 * 1024 * 1024


def _resident(shape):
    nd = len(shape)
    return pl.BlockSpec(shape, lambda *_: (0,) * nd, pipeline_mode=pl.Buffered(1))


def _rmsnorm(x, g):
    return x * lax.rsqrt(jnp.mean(x * x, axis=-1, keepdims=True) + NORM_EPS) * g


def _sigmoid(x):
    return 0.5 * jnp.tanh(0.5 * x) + 0.5


def _bdot(a, b):
    return jnp.dot(a, b, preferred_element_type=F32)


GELU_K = (2.0 / 3.141592653589793) ** 0.5


def _gelu_tanh(x):
    inner = x * ((GELU_K * 0.044715) * (x * x) + GELU_K)
    return x * (0.5 * jnp.tanh(inner) + 0.5)


def _shift_rows(padded, back):
    return pltpu.roll(padded, back, axis=0)[SUBLANES:]


def _in_lru_kernel(x_ref, g_ref, w_ref, cw_ref, cb_ref, wr_ref, br_ref, wi_ref, bi_ref, lam_ref,
                   *rest):
    n_cast = (len(rest) - 11) // 2
    cast_srcs, rest = rest[:n_cast], rest[n_cast:]
    q_ref, k_ref, v_ref, gl_ref, ga_ref, y_ref = rest[:6]
    cast_dsts, (tail, a_scr, u_scr, h_scr, hprev) = rest[6:6 + n_cast], rest[6 + n_cast:]
    ts, d = x_ref.shape
    for src, dst in zip(cast_srcs, cast_dsts):
        dst[...] = src[...].astype(dst.dtype)

    @pl.when(pl.program_id(1) == 0)
    def _():
        tail[...] = jnp.zeros(tail.shape, F32)
        hprev[...] = jnp.zeros(hprev.shape, F32)

    h = _rmsnorm(x_ref[...], g_ref[...]).astype(BF16)

    def proj(n):
        return _bdot(h, w_ref[:, n * d:(n + 1) * d])

    outs = {2: q_ref, 3: k_ref, 4: v_ref, 5: gl_ref, 6: ga_ref}

    def emit(n):
        outs[n][...] = proj(n).astype(outs[n].dtype)

    ux = proj(0)
    ug = proj(1)
    row = lax.broadcasted_iota(jnp.int32, (SUBLANES, d), 0)
    padded = jnp.concatenate([tail[...], ux], axis=0)
    c = cb_ref[...]
    for k in range(CONV_W):
        back = CONV_W - 1 - k
        xs = ux if back == 0 else _shift_rows(padded, back)
        c = c + cw_ref[k:k + 1, :] * xs
    tail[...] = ux[ts - SUBLANES:, :]
    emit(2)

    cb = c.astype(BF16)
    gr, gi = [], []
    for j in range(d // MXU_DIM):
        sl = slice(j * MXU_DIM, (j + 1) * MXU_DIM)
        gr.append(_bdot(cb[:, sl], wr_ref[j]))
        gi.append(_bdot(cb[:, sl], wi_ref[j]))
    ig = _sigmoid(jnp.concatenate(gi, axis=1) + bi_ref[...])
    emit(3)

    neg_lam = -lam_ref[...]
    softplus = jnp.maximum(neg_lam, 0.0) + jnp.log1p(jnp.exp(-jnp.abs(neg_lam)))
    half_scale = (0.5 * LRU_C) * softplus
    neg_log_a = (half_scale * jnp.tanh(0.5 * (jnp.concatenate(gr, axis=1) + br_ref[...]))
                 + half_scale)
    a = jnp.exp2(neg_log_a * (-LOG2E))
    one_m_a2 = jnp.tanh(neg_log_a) * (a * a + 1.0)
    mult = jnp.where(one_m_a2 > 0.0, one_m_a2 * lax.rsqrt(one_m_a2), 0.0)
    a_scr[...] = a
    u_scr[...] = mult * (ig * c)
    emit(4)

    h_in = hprev[...]
    for g0 in range(0, ts, SUBLANES):
        if g0 == ts // 2:
            emit(5)
        a8 = a_scr[g0:g0 + SUBLANES, :]
        u8 = u_scr[g0:g0 + SUBLANES, :]
        for dist in (1, 2, 4):
            keep = row >= dist
            u_sh = pltpu.roll(u8, dist, axis=0)
            a_sh = pltpu.roll(a8, dist, axis=0)
            u8 = jnp.where(keep, a8 * u_sh + u8, u8)
            a8 = jnp.where(keep, a8 * a_sh, a8)
        h8 = a8 * h_in + u8
        h_scr[g0:g0 + SUBLANES, :] = h8
        h_in = h8[SUBLANES - 1:SUBLANES, :]
    hprev[...] = h_in
    emit(6)

    y_ref[...] = (h_scr[...] * _gelu_tanh(ug)).astype(y_ref.dtype)


def _block_diag(wb, group):
    n, c, _ = wb.shape
    wg = wb.reshape(n // group, group, c, c)
    eye = jnp.eye(group, dtype=wb.dtype)
    dense = wg[:, :, :, None, :] * eye[None, :, None, :, None]
    return dense.reshape(n // group, group * c, group * c)


def _in_lru(x2, g, w_bf, conv_w, conv_b, w_r, b_r, w_i, b_i, lam, later_weights, batch, ts):
    t, d = x2.shape
    nt = t // batch // ts
    steps = batch * nt
    assert all(w.shape[0] % steps == 0 for w in later_weights), "slabs must tile the weights"
    slabs = [pl.BlockSpec((w.shape[0] // steps, w.shape[1]), lambda b, i: (b * nt + i, 0))
             for w in later_weights]
    group = MXU_DIM // w_r.shape[1]
    wr = _block_diag(w_r, group).astype(BF16)
    wi = _block_diag(w_i, group).astype(BF16)
    tile = pl.BlockSpec((ts, d), lambda b, i: (b * nt + i, 0))
    row = _resident((1, d))
    dtypes = [BF16, BF16, BF16, F32, F32, BF16]
    outs = pl.pallas_call(
        _in_lru_kernel,
        grid=(batch, nt),
        in_specs=[tile, row, _resident(w_bf.shape), _resident(conv_w.shape), row,
                  _resident(wr.shape), row, _resident(wi.shape), row, row] + slabs,
        out_specs=[tile] * len(dtypes) + slabs,
        out_shape=([jax.ShapeDtypeStruct((t, d), dt) for dt in dtypes]
                   + [jax.ShapeDtypeStruct(w.shape, BF16) for w in later_weights]),
        scratch_shapes=[pltpu.VMEM((SUBLANES, d), F32), pltpu.VMEM((ts, d), F32),
                        pltpu.VMEM((ts, d), F32), pltpu.VMEM((ts, d), F32),
                        pltpu.VMEM((1, d), F32)],
        compiler_params=pltpu.CompilerParams(
            dimension_semantics=("arbitrary", "arbitrary"), vmem_limit_bytes=VMEM_LIMIT),
        name="in_lru",
    )(x2, g, w_bf, conv_w, conv_b.reshape(1, d), wr, b_r.reshape(1, d), wi, b_i.reshape(1, d),
      lam.reshape(1, d), *later_weights)
    return outs[:len(dtypes)], outs[len(dtypes):]


def _softplus(z):
    return jnp.maximum(z, 0.0) + jnp.log(1.0 + jnp.exp2(jnp.abs(z) * (-LOG2E)))


LANE_BLOCKS = 4


def _attn_kernel(qp_ref, qn_ref, k_ref, v_ref, tri_ref, o_ref, z_scr, att_scr, carry_scr):
    tq = qp_ref.shape[0]
    tk = tri_ref.shape[1]
    i = pl.program_id(2)
    nq = pl.num_programs(2) - 1
    lane = lax.broadcasted_iota(jnp.int32, (tq, ATTN_LANES), 1)
    tri = tri_ref[...]
    done = jnp.maximum(i - 1, 0)

    def stack_heads(q):
        q = q * (HEAD_DIM ** -0.5)
        zero = jnp.zeros_like(q)
        return jnp.concatenate(
            [jnp.where(lane // HEAD_DIM == h, q, zero) for h in range(HEADS_PER_BLOCK)], axis=0)

    def tile_rows(j):
        return pl.ds(pl.multiple_of(j * tk, tk), tk)

    def scores(q2, k_tiles):
        return lax.dot_general(q2, k_tiles, (((1,), (1,)), ((), ())), preferred_element_type=F32)

    def decay_terms(z):
        pos = _softplus(z)
        pos_bf = pos.astype(BF16)
        later = _bdot(pos_bf, tri)
        return z - pos, later, later[:, 0:1] + pos_bf[:, 0:1].astype(F32)

    def first_scores(q2, j, lanes):
        prev = jnp.maximum(j - 1, 0)
        return scores(q2, jnp.concatenate(
            [k_ref[tile_rows(j), lanes], k_ref[tile_rows(prev), lanes]], axis=0))

    @pl.when(i == 0)
    def _():
        for lb in range(LANE_BLOCKS):
            lanes = slice(lb * ATTN_LANES, (lb + 1) * ATTN_LANES)
            z_scr[lb] = first_scores(stack_heads(qp_ref[:, lanes]), 0, lanes)
        att_scr[...] = jnp.zeros(att_scr.shape, BF16)
        carry_scr[...] = jnp.full(carry_scr.shape, 2.0 * EXP_UNDERFLOW, F32)

    def first_steps(lb):
        lanes = slice(lb * ATTN_LANES, (lb + 1) * ATTN_LANES)

        carry_done = carry_scr[lb]
        v_prev = v_ref[tile_rows(jnp.maximum(done - 1, 0)), lanes]
        v_both = jnp.concatenate(
            [v_ref[tile_rows(done), lanes], jnp.where(done > 0, v_prev, jnp.zeros_like(v_prev))],
            axis=0)
        acc = _bdot(att_scr[lb], v_both)

        z = z_scr[lb]
        z_scr[lb] = first_scores(stack_heads(qn_ref[:, lanes]), jnp.minimum(i + 1, nq - 1),
                                 lanes)
        rq = lax.broadcasted_iota(jnp.int32, (HEADS_PER_BLOCK * tq, tk), 0) % tq
        ck = lax.broadcasted_iota(jnp.int32, (HEADS_PER_BLOCK * tq, tk), 1)
        z_diag = jnp.where(ck < rq, z[:, :tk], MASKED_LOGIT)
        z_prev = z[:, tk:]
        rows = HEADS_PER_BLOCK * tq
        log_beta, later, total = decay_terms(jnp.concatenate([z_diag, z_prev], axis=0))
        carry = total[:rows]
        att_scr[lb] = jnp.concatenate(
            [jnp.exp(log_beta[:rows] - later[:rows]),
             jnp.exp(log_beta[rows:] - (later[rows:] + carry))], axis=1).astype(BF16)
        carry_scr[lb] = carry + total[rows:]
        return acc, carry_done

    def earlier_tiles(lb, acc, carry_done):
        lanes = slice(lb * ATTN_LANES, (lb + 1) * ATTN_LANES)

        def cond(state):
            j, low, _, _ = state
            return jnp.logical_and(j >= 0, low < EXP_UNDERFLOW)

        def body(state):
            j, _, acc, carry = state
            z = scores(stack_heads(qp_ref[:, lanes]), k_ref[tile_rows(j), lanes])
            log_beta, later, total = decay_terms(z)
            att = jnp.exp(log_beta - (later + carry))
            acc = acc + _bdot(att.astype(BF16), v_ref[tile_rows(j), lanes])
            carry = carry + total
            return j - 1, jnp.min(carry), acc, carry

        _, _, acc, _ = lax.while_loop(cond, body, (i - 3, jnp.min(carry_done), acc, carry_done))
        out = acc[:tq]
        for h in range(1, HEADS_PER_BLOCK):
            out = jnp.where(lane // HEAD_DIM == h, acc[h * tq:(h + 1) * tq], out)
        o_ref[:, lanes] = out.astype(o_ref.dtype)

    started = [first_steps(lb) for lb in range(LANE_BLOCKS)]
    for lb, (acc, carry_done) in enumerate(started):
        earlier_tiles(lb, acc, carry_done)


def _attn(q, k, v, batch, tq):
    t, width = q.shape
    seq = t // batch
    nq = seq // tq
    tk = tq
    group = LANE_BLOCKS * ATTN_LANES
    later = jnp.arange(tk)[:, None] > jnp.arange(tk)[None, :]

    def q_tile(shift):
        return pl.BlockSpec(
            (tq, group), lambda b, hp, i: (b * nq + jnp.clip(i + shift, 0, nq - 1), hp))

    kv_spec = pl.BlockSpec((seq, group), lambda b, hp, i: (b, hp))
    rows = HEADS_PER_BLOCK * tq
    return pl.pallas_call(
        _attn_kernel,
        grid=(batch, width // group, nq + 1),
        in_specs=[q_tile(-1), q_tile(1), kv_spec, kv_spec, _resident((tk, tk))],
        out_specs=q_tile(-1),
        out_shape=jax.ShapeDtypeStruct((t, width), BF16),
        scratch_shapes=[pltpu.VMEM((LANE_BLOCKS, rows, 2 * tk), F32),
                        pltpu.VMEM((LANE_BLOCKS, rows, 2 * tk), BF16),
                        pltpu.VMEM((LANE_BLOCKS, rows, 1), F32)],
        compiler_params=pltpu.CompilerParams(
            dimension_semantics=("arbitrary", "arbitrary", "arbitrary"),
            vmem_limit_bytes=VMEM_LIMIT),
        name="attn",
    )(q, q, k, v, later.astype(BF16))


def _post_kernel(x_ref, yl_ref, ya_ref, gl_ref, ga_ref, p_ref, wbl_ref, wba_ref, wo_ref,
                 g_mlp_ref, wu_ref, wd_ref, g_ple_ref, wpg_ref, wpl_ref, g_fin_ref, o_ref):
    merged = (_sigmoid(gl_ref[...]) * _bdot(yl_ref[...], wbl_ref[...])
              + _sigmoid(ga_ref[...]) * _bdot(ya_ref[...], wba_ref[...]))
    x = x_ref[...] + _bdot(merged.astype(BF16), wo_ref[...])
    h2 = _rmsnorm(x, g_mlp_ref[...]).astype(BF16)
    up = jnp.square(jnp.maximum(_bdot(h2, wu_ref[...]), 0.0))
    x = x + _bdot(up.astype(BF16), wd_ref[...])
    h3 = _rmsnorm(x, g_ple_ref[...]).astype(BF16)
    gate = _sigmoid(_bdot(h3, wpg_ref[...]))
    x = x + gate * _bdot(p_ref[...].astype(BF16), wpl_ref[...])
    o_ref[...] = _rmsnorm(x, g_fin_ref[...])


def _post(x2, y_lru, y_att, g_lru, g_att, p2, w_bl, w_ba, w_o, g_mlp, w_up, w_down, g_ple, w_pg,
          w_pl, g_fin, tm):
    t, d = x2.shape
    tile = pl.BlockSpec((tm, d), lambda i: (i, 0))
    ptile = pl.BlockSpec((tm, p2.shape[1]), lambda i: (i, 0))
    row = _resident((1, d))
    square = _resident((d, d))
    return pl.pallas_call(
        _post_kernel,
        grid=(t // tm,),
        in_specs=[tile] * 5 + [ptile, square, square, square, row, _resident(w_up.shape),
                               _resident(w_down.shape), row, square, _resident(w_pl.shape), row],
        out_specs=tile,
        out_shape=jax.ShapeDtypeStruct((t, d), F32),
        compiler_params=pltpu.CompilerParams(
            dimension_semantics=("arbitrary",), vmem_limit_bytes=VMEM_LIMIT),
        name="post",
    )(x2, y_lru, y_att, g_lru, g_att, p2, w_bl, w_ba, w_o, g_mlp, w_up, w_down, g_ple, w_pg,
      w_pl, g_fin)


def kernel(x, p, norm_mix_g, w_in, conv_w, conv_b, w_rgate, b_rgate, w_igate, b_igate, lru_lambda, w_br_lru, w_br_att, w_out, norm_mlp_g, w_mlp_up, w_mlp_down, norm_ple_g, w_ple_gate, w_ple, norm_final_g):
    batch, seq, d = x.shape
    depth = w_in.shape[0]
    t = batch * seq
    x2 = x.reshape(t, d)
    for l in range(depth):
        later = [w_br_lru[l], w_br_att[l], w_out[l], w_mlp_up[l], w_mlp_down[l], w_ple_gate[l]]
        (q, k, v, g_lru, g_att, y_lru), later_bf = _in_lru(
            x2, norm_mix_g[l].reshape(1, d), w_in[l].astype(BF16), conv_w[l], conv_b[l],
            w_rgate[l], b_rgate[l], w_igate[l], b_igate[l], lru_lambda[l], later, batch, ts=512)
        w_bl, w_ba, w_o, w_up, w_down, w_pg = later_bf
        y_att = _attn(q, k, v, batch, tq=256)
        assert l == depth - 1, "deeper stacks need a post variant without the final norm"
        x2 = _post(x2, y_lru, y_att, g_lru, g_att, p[l].reshape(t, -1), w_bl, w_ba, w_o,
                   norm_mlp_g[l].reshape(1, d), w_up, w_down, norm_ple_g[l].reshape(1, d),
                   w_pg, w_ple[l].astype(BF16), norm_final_g.reshape(1, d), tm=512)
    return x2.reshape(batch, seq, d)
```

```python
import jax
import jax.numpy as jnp
from jax import lax
from jax.experimental import pallas as pl
from jax.experimental.pallas import tpu as pltpu

F32 = jnp.float32
BF16 = jnp.bfloat16

NORM_EPS = 1e-6
LRU_C = 8.0
CONV_W = 4
HEAD_DIM = 64
SUBLANES = 8
MXU_DIM = 256
ATTN_LANES = 128
HEADS_PER_BLOCK = ATTN_LANES // HEAD_DIM

LOG2E = 1.4426950408889634
EXP_UNDERFLOW = 105.0
MASKED_LOGIT = -1e30

VMEM_LIMIT = 60 * 1024 * 1024


def _resident(shape):
    nd = len(shape)
    return pl.BlockSpec(shape, lambda *_: (0,) * nd, pipeline_mode=pl.Buffered(1))


def _rmsnorm(x, g):
    return x * lax.rsqrt(jnp.mean(x * x, axis=-1, keepdims=True) + NORM_EPS) * g


def _sigmoid(x):
    return 0.5 * jnp.tanh(0.5 * x) + 0.5


def _bdot(a, b):
    return jnp.dot(a, b, preferred_element_type=F32)


GELU_K = (2.0 / 3.141592653589793) ** 0.5


def _gelu_tanh(x):
    inner = x * ((GELU_K * 0.044715) * (x * x) + GELU_K)
    return x * (0.5 * jnp.tanh(inner) + 0.5)


def _shift_rows(padded, back):
    return pltpu.roll(padded, back, axis=0)[SUBLANES:]


def _in_lru_kernel(x_ref, g_ref, w_ref, cw_ref, cb_ref, wr_ref, br_ref, wi_ref, bi_ref, lam_ref,
                   *rest):
    n_cast = (len(rest) - 11) // 2
    cast_srcs, rest = rest[:n_cast], rest[n_cast:]
    q_ref, k_ref, v_ref, gl_ref, ga_ref, y_ref = rest[:6]
    cast_dsts, (tail, a_scr, u_scr, h_scr, hprev) = rest[6:6 + n_cast], rest[6 + n_cast:]
    ts, d = x_ref.shape
    for src, dst in zip(cast_srcs, cast_dsts):
        dst[...] = src[...].astype(dst.dtype)

    @pl.when(pl.program_id(1) == 0)
    def _():
        tail[...] = jnp.zeros(tail.shape, F32)
        hprev[...] = jnp.zeros(hprev.shape, F32)

    h = _rmsnorm(x_ref[...], g_ref[...]).astype(BF16)

    def proj(n):
        return _bdot(h, w_ref[:, n * d:(n + 1) * d])

    outs = {2: q_ref, 3: k_ref, 4: v_ref, 5: gl_ref, 6: ga_ref}

    def emit(n):
        outs[n][...] = proj(n).astype(outs[n].dtype)

    ux = proj(0)
    ug = proj(1)
    row = lax.broadcasted_iota(jnp.int32, (SUBLANES, d), 0)
    padded = jnp.concatenate([tail[...], ux], axis=0)
    c = cb_ref[...]
    for k in range(CONV_W):
        back = CONV_W - 1 - k
        xs = ux if back == 0 else _shift_rows(padded, back)
        c = c + cw_ref[k:k + 1, :] * xs
    tail[...] = ux[ts - SUBLANES:, :]
    emit(2)

    cb = c.astype(BF16)
    gr, gi = [], []
    for j in range(d // MXU_DIM):
        sl = slice(j * MXU_DIM, (j + 1) * MXU_DIM)
        gr.append(_bdot(cb[:, sl], wr_ref[j]))
        gi.append(_bdot(cb[:, sl], wi_ref[j]))
    ig = _sigmoid(jnp.concatenate(gi, axis=1) + bi_ref[...])
    emit(3)

    neg_lam = -lam_ref[...]
    softplus = jnp.maximum(neg_lam, 0.0) + jnp.log1p(jnp.exp(-jnp.abs(neg_lam)))
    half_scale = (0.5 * LRU_C) * softplus
    neg_log_a = (half_scale * jnp.tanh(0.5 * (jnp.concatenate(gr, axis=1) + br_ref[...]))
                 + half_scale)
    a = jnp.exp2(neg_log_a * (-LOG2E))
    one_m_a2 = jnp.tanh(neg_log_a) * (a * a + 1.0)
    mult = jnp.where(one_m_a2 > 0.0, one_m_a2 * lax.rsqrt(one_m_a2), 0.0)
    a_scr[...] = a
    u_scr[...] = mult * (ig * c)
    emit(4)

    h_in = hprev[...]
    for g0 in range(0, ts, SUBLANES):
        if g0 == ts // 2:
            emit(5)
        a8 = a_scr[g0:g0 + SUBLANES, :]
        u8 = u_scr[g0:g0 + SUBLANES, :]
        for dist in (1, 2, 4):
            keep = row >= dist
            u_sh = pltpu.roll(u8, dist, axis=0)
            a_sh = pltpu.roll(a8, dist, axis=0)
            u8 = jnp.where(keep, a8 * u_sh + u8, u8)
            a8 = jnp.where(keep, a8 * a_sh, a8)
        h8 = a8 * h_in + u8
        h_scr[g0:g0 + SUBLANES, :] = h8
        h_in = h8[SUBLANES - 1:SUBLANES, :]
    hprev[...] = h_in
    emit(6)

    y_ref[...] = (h_scr[...] * _gelu_tanh(ug)).astype(y_ref.dtype)


def _block_diag(wb, group):
    n, c, _ = wb.shape
    wg = wb.reshape(n // group, group, c, c)
    eye = jnp.eye(group, dtype=wb.dtype)
    dense = wg[:, :, :, None, :] * eye[None, :, None, :, None]
    return dense.reshape(n // group, group * c, group * c)


def _in_lru(x2, g, w_bf, conv_w, conv_b, w_r, b_r, w_i, b_i, lam, later_weights, batch, ts):
    t, d = x2.shape
    nt = t // batch // ts
    steps = batch * nt
    assert all(w.shape[0] % steps == 0 for w in later_weights), "slabs must tile the weights"
    slabs = [pl.BlockSpec((w.shape[0] // steps, w.shape[1]), lambda b, i: (b * nt + i, 0))
             for w in later_weights]
    group = MXU_DIM // w_r.shape[1]
    wr = _block_diag(w_r, group).astype(BF16)
    wi = _block_diag(w_i, group).astype(BF16)
    tile = pl.BlockSpec((ts, d), lambda b, i: (b * nt + i, 0))
    row = _resident((1, d))
    dtypes = [BF16, BF16, BF16, F32, F32, BF16]
    outs = pl.pallas_call(
        _in_lru_kernel,
        grid=(batch, nt),
        in_specs=[tile, row, _resident(w_bf.shape), _resident(conv_w.shape), row,
                  _resident(wr.shape), row, _resident(wi.shape), row, row] + slabs,
        out_specs=[tile] * len(dtypes) + slabs,
        out_shape=([jax.ShapeDtypeStruct((t, d), dt) for dt in dtypes]
                   + [jax.ShapeDtypeStruct(w.shape, BF16) for w in later_weights]),
        scratch_shapes=[pltpu.VMEM((SUBLANES, d), F32), pltpu.VMEM((ts, d), F32),
                        pltpu.VMEM((ts, d), F32), pltpu.VMEM((ts, d), F32),
                        pltpu.VMEM((1, d), F32)],
        compiler_params=pltpu.CompilerParams(
            dimension_semantics=("arbitrary", "arbitrary"), vmem_limit_bytes=VMEM_LIMIT),
        name="in_lru",
    )(x2, g, w_bf, conv_w, conv_b.reshape(1, d), wr, b_r.reshape(1, d), wi, b_i.reshape(1, d),
      lam.reshape(1, d), *later_weights)
    return outs[:len(dtypes)], outs[len(dtypes):]


def _softplus(z):
    return jnp.maximum(z, 0.0) + jnp.log(1.0 + jnp.exp2(jnp.abs(z) * (-LOG2E)))


LANE_BLOCKS = 4


def _attn_kernel(qp_ref, qn_ref, k_ref, v_ref, tri_ref, o_ref, z_scr, att_scr, carry_scr):
    tq = qp_ref.shape[0]
    tk = tri_ref.shape[1]
    i = pl.program_id(2)
    nq = pl.num_programs(2) - 1
    lane = lax.broadcasted_iota(jnp.int32, (tq, ATTN_LANES), 1)
    tri = tri_ref[...]
    done = jnp.maximum(i - 1, 0)

    def stack_heads(q):
        q = q * (HEAD_DIM ** -0.5)
        zero = jnp.zeros_like(q)
        return jnp.concatenate(
            [jnp.where(lane // HEAD_DIM == h, q, zero) for h in range(HEADS_PER_BLOCK)], axis=0)

    def tile_rows(j):
        return pl.ds(pl.multiple_of(j * tk, tk), tk)

    def scores(q2, k_tiles):
        return lax.dot_general(q2, k_tiles, (((1,), (1,)), ((), ())), preferred_element_type=F32)

    def decay_terms(z):
        pos = _softplus(z)
        pos_bf = pos.astype(BF16)
        later = _bdot(pos_bf, tri)
        return z - pos, later, later[:, 0:1] + pos_bf[:, 0:1].astype(F32)

    def first_scores(q2, j, lanes):
        prev = jnp.maximum(j - 1, 0)
        return scores(q2, jnp.concatenate(
            [k_ref[tile_rows(j), lanes], k_ref[tile_rows(prev), lanes]], axis=0))

    @pl.when(i == 0)
    def _():
        for lb in range(LANE_BLOCKS):
            lanes = slice(lb * ATTN_LANES, (lb + 1) * ATTN_LANES)
            z_scr[lb] = first_scores(stack_heads(qp_ref[:, lanes]), 0, lanes)
        att_scr[...] = jnp.zeros(att_scr.shape, BF16)
        carry_scr[...] = jnp.full(carry_scr.shape, 2.0 * EXP_UNDERFLOW, F32)

    def first_steps(lb):
        lanes = slice(lb * ATTN_LANES, (lb + 1) * ATTN_LANES)

        carry_done = carry_scr[lb]
        v_prev = v_ref[tile_rows(jnp.maximum(done - 1, 0)), lanes]
        v_both = jnp.concatenate(
            [v_ref[tile_rows(done), lanes], jnp.where(done > 0, v_prev, jnp.zeros_like(v_prev))],
            axis=0)
        acc = _bdot(att_scr[lb], v_both)

        z = z_scr[lb]
        z_scr[lb] = first_scores(stack_heads(qn_ref[:, lanes]), jnp.minimum(i + 1, nq - 1),
                                 lanes)
        rq = lax.broadcasted_iota(jnp.int32, (HEADS_PER_BLOCK * tq, tk), 0) % tq
        ck = lax.broadcasted_iota(jnp.int32, (HEADS_PER_BLOCK * tq, tk), 1)
        z_diag = jnp.where(ck < rq, z[:, :tk], MASKED_LOGIT)
        z_prev = z[:, tk:]
        rows = HEADS_PER_BLOCK * tq
        log_beta, later, total = decay_terms(jnp.concatenate([z_diag, z_prev], axis=0))
        carry = total[:rows]
        att_scr[lb] = jnp.concatenate(
            [jnp.exp(log_beta[:rows] - later[:rows]),
             jnp.exp(log_beta[rows:] - (later[rows:] + carry))], axis=1).astype(BF16)
        carry_scr[lb] = carry + total[rows:]
        return acc, carry_done, jnp.min(carry_done)

    def earlier_tiles(lb, acc, carry_done, low):
        lanes = slice(lb * ATTN_LANES, (lb + 1) * ATTN_LANES)

        def cond(state):
            j, low, _, _ = state
            return jnp.logical_and(j >= 0, low < EXP_UNDERFLOW)

        def body(state):
            j, _, acc, carry = state
            z = scores(stack_heads(qp_ref[:, lanes]), k_ref[tile_rows(j), lanes])
            log_beta, later, total = decay_terms(z)
            att = jnp.exp(log_beta - (later + carry))
            acc = acc + _bdot(att.astype(BF16), v_ref[tile_rows(j), lanes])
            carry = carry + total
            return j - 1, jnp.min(carry), acc, carry

        _, _, acc, _ = lax.while_loop(cond, body, (i - 3, low, acc, carry_done))
        out = acc[:tq]
        for h in range(1, HEADS_PER_BLOCK):
            out = jnp.where(lane // HEAD_DIM == h, acc[h * tq:(h + 1) * tq], out)
        o_ref[:, lanes] = out.astype(o_ref.dtype)

    started = [first_steps(lb) for lb in range(LANE_BLOCKS)]
    for lb, (acc, carry_done, low) in enumerate(started):
        earlier_tiles(lb, acc, carry_done, low)


def _attn(q, k, v, batch, tq):
    t, width = q.shape
    seq = t // batch
    nq = seq // tq
    tk = tq
    group = LANE_BLOCKS * ATTN_LANES
    later = jnp.arange(tk)[:, None] > jnp.arange(tk)[None, :]

    def q_tile(shift):
        return pl.BlockSpec(
            (tq, group), lambda b, hp, i: (b * nq + jnp.clip(i + shift, 0, nq - 1), hp))

    kv_spec = pl.BlockSpec((seq, group), lambda b, hp, i: (b, hp))
    rows = HEADS_PER_BLOCK * tq
    return pl.pallas_call(
        _attn_kernel,
        grid=(batch, width // group, nq + 1),
        in_specs=[q_tile(-1), q_tile(1), kv_spec, kv_spec, _resident((tk, tk))],
        out_specs=q_tile(-1),
        out_shape=jax.ShapeDtypeStruct((t, width), BF16),
        scratch_shapes=[pltpu.VMEM((LANE_BLOCKS, rows, 2 * tk), F32),
                        pltpu.VMEM((LANE_BLOCKS, rows, 2 * tk), BF16),
                        pltpu.VMEM((LANE_BLOCKS, rows, 1), F32)],
        compiler_params=pltpu.CompilerParams(
            dimension_semantics=("arbitrary", "arbitrary", "arbitrary"),
            vmem_limit_bytes=VMEM_LIMIT),
        name="attn",
    )(q, q, k, v, later.astype(BF16))


def _post_kernel(x_ref, yl_ref, ya_ref, gl_ref, ga_ref, p_ref, wbl_ref, wba_ref, wo_ref,
                 g_mlp_ref, wu_ref, wd_ref, g_ple_ref, wpg_ref, wpl_ref, g_fin_ref, o_ref):
    merged = (_sigmoid(gl_ref[...]) * _bdot(yl_ref[...], wbl_ref[...])
              + _sigmoid(ga_ref[...]) * _bdot(ya_ref[...], wba_ref[...]))
    x = x_ref[...] + _bdot(merged.astype(BF16), wo_ref[...])
    h2 = _rmsnorm(x, g_mlp_ref[...]).astype(BF16)
    up = jnp.square(jnp.maximum(_bdot(h2, wu_ref[...]), 0.0))
    x = x + _bdot(up.astype(BF16), wd_ref[...])
    h3 = _rmsnorm(x, g_ple_ref[...]).astype(BF16)
    gate = _sigmoid(_bdot(h3, wpg_ref[...]))
    x = x + gate * _bdot(p_ref[...].astype(BF16), wpl_ref[...])
    o_ref[...] = _rmsnorm(x, g_fin_ref[...])


def _post(x2, y_lru, y_att, g_lru, g_att, p2, w_bl, w_ba, w_o, g_mlp, w_up, w_down, g_ple, w_pg,
          w_pl, g_fin, tm):
    t, d = x2.shape
    tile = pl.BlockSpec((tm, d), lambda i: (i, 0))
    ptile = pl.BlockSpec((tm, p2.shape[1]), lambda i: (i, 0))
    row = _resident((1, d))
    square = _resident((d, d))
    return pl.pallas_call(
        _post_kernel,
        grid=(t // tm,),
        in_specs=[tile] * 5 + [ptile, square, square, square, row, _resident(w_up.shape),
                               _resident(w_down.shape), row, square, _resident(w_pl.shape), row],
        out_specs=tile,
        out_shape=jax.ShapeDtypeStruct((t, d), F32),
        compiler_params=pltpu.CompilerParams(
            dimension_semantics=("arbitrary",), vmem_limit_bytes=VMEM_LIMIT),
        name="post",
    )(x2, y_lru, y_att, g_lru, g_att, p2, w_bl, w_ba, w_o, g_mlp, w_up, w_down, g_ple, w_pg,
      w_pl, g_fin)


def kernel(x, p, norm_mix_g, w_in, conv_w, conv_b, w_rgate, b_rgate, w_igate, b_igate, lru_lambda, w_br_lru, w_br_att, w_out, norm_mlp_g, w_mlp_up, w_mlp_down, norm_ple_g, w_ple_gate, w_ple, norm_final_g):
    batch, seq, d = x.shape
    depth = w_in.shape[0]
    t = batch * seq
    x2 = x.reshape(t, d)
    for l in range(depth):
        later = [w_br_lru[l], w_br_att[l], w_out[l], w_mlp_up[l], w_mlp_down[l], w_ple_gate[l]]
        (q, k, v, g_lru, g_att, y_lru), later_bf = _in_lru(
            x2, norm_mix_g[l].reshape(1, d), w_in[l].astype(BF16), conv_w[l], conv_b[l],
            w_rgate[l], b_rgate[l], w_igate[l], b_igate[l], lru_lambda[l], later, batch, ts=512)
        w_bl, w_ba, w_o, w_up, w_down, w_pg = later_bf
        y_att = _attn(q, k, v, batch, tq=256)
        assert l == depth - 1, "deeper stacks need a post variant without the final norm"
        x2 = _post(x2, y_lru, y_att, g_lru, g_att, p[l].reshape(t, -1), w_bl, w_ba, w_o,
                   norm_mlp_g[l].reshape(1, d), w_up, w_down, norm_ple_g[l].reshape(1, d),
                   w_pg, w_ple[l].astype(BF16), norm_final_g.reshape(1, d), tm=512)
    return x2.reshape(batch, seq, d)
```

```python
import jax
import jax.numpy as jnp
from jax import lax
from jax.experimental import pallas as pl
from jax.experimental.pallas import tpu as pltpu

F32 = jnp.float32
BF16 = jnp.bfloat16

NORM_EPS = 1e-6
LRU_C = 8.0
CONV_W = 4
HEAD_DIM = 64
SUBLANES = 8
MXU_DIM = 256
ATTN_LANES = 128
HEADS_PER_BLOCK = ATTN_LANES // HEAD_DIM

LOG2E = 1.4426950408889634
EXP_UNDERFLOW = 105.0
MASKED_LOGIT = -1e30

VMEM_LIMIT = 60 * 1024 * 1024


def _resident(shape):
    nd = len(shape)
    return pl.BlockSpec(shape, lambda *_: (0,) * nd, pipeline_mode=pl.Buffered(1))


def _rmsnorm(x, g):
    return x * lax.rsqrt(jnp.mean(x * x, axis=-1, keepdims=True) + NORM_EPS) * g


def _sigmoid(x):
    return 0.5 * jnp.tanh(0.5 * x) + 0.5


def _bdot(a, b):
    return jnp.dot(a, b, preferred_element_type=F32)


GELU_K = (2.0 / 3.141592653589793) ** 0.5


def _gelu_tanh(x):
    inner = x * ((GELU_K * 0.044715) * (x * x) + GELU_K)
    return x * (0.5 * jnp.tanh(inner) + 0.5)


def _shift_rows(padded, back):
    return pltpu.roll(padded, back, axis=0)[SUBLANES:]


def _in_lru_kernel(x_ref, g_ref, w_ref, cw_ref, cb_ref, wr_ref, br_ref, wi_ref, bi_ref, lam_ref,
                   *rest):
    n_cast = (len(rest) - 11) // 2
    cast_srcs, rest = rest[:n_cast], rest[n_cast:]
    q_ref, k_ref, v_ref, gl_ref, ga_ref, y_ref = rest[:6]
    cast_dsts, (tail, a_scr, u_scr, h_scr, hprev) = rest[6:6 + n_cast], rest[6 + n_cast:]
    ts, d = x_ref.shape
    for src, dst in zip(cast_srcs, cast_dsts):
        dst[...] = src[...].astype(dst.dtype)

    @pl.when(pl.program_id(1) == 0)
    def _():
        tail[...] = jnp.zeros(tail.shape, F32)
        hprev[...] = jnp.zeros(hprev.shape, F32)

    h = _rmsnorm(x_ref[...], g_ref[...]).astype(BF16)

    def proj(n):
        return _bdot(h, w_ref[:, n * d:(n + 1) * d])

    outs = {2: q_ref, 3: k_ref, 4: v_ref, 5: gl_ref, 6: ga_ref}

    def emit(n):
        outs[n][...] = proj(n).astype(outs[n].dtype)

    ux = proj(0)
    ug = proj(1)
    row = lax.broadcasted_iota(jnp.int32, (SUBLANES, d), 0)
    padded = jnp.concatenate([tail[...], ux], axis=0)
    c = cb_ref[...]
    for k in range(CONV_W):
        back = CONV_W - 1 - k
        xs = ux if back == 0 else _shift_rows(padded, back)
        c = c + cw_ref[k:k + 1, :] * xs
    tail[...] = ux[ts - SUBLANES:, :]
    emit(2)

    cb = c.astype(BF16)
    gr, gi = [], []
    for j in range(d // MXU_DIM):
        sl = slice(j * MXU_DIM, (j + 1) * MXU_DIM)
        gr.append(_bdot(cb[:, sl], wr_ref[j]))
        gi.append(_bdot(cb[:, sl], wi_ref[j]))
    ig = _sigmoid(jnp.concatenate(gi, axis=1) + bi_ref[...])
    emit(3)

    neg_lam = -lam_ref[...]
    softplus = jnp.maximum(neg_lam, 0.0) + jnp.log1p(jnp.exp(-jnp.abs(neg_lam)))
    half_scale = (0.5 * LRU_C) * softplus
    neg_log_a = (half_scale * jnp.tanh(0.5 * (jnp.concatenate(gr, axis=1) + br_ref[...]))
                 + half_scale)
    a = jnp.exp2(neg_log_a * (-LOG2E))
    one_m_a2 = jnp.tanh(neg_log_a) * (a * a + 1.0)
    mult = jnp.where(one_m_a2 > 0.0, one_m_a2 * lax.rsqrt(one_m_a2), 0.0)
    a_scr[...] = a
    u_scr[...] = mult * (ig * c)
    emit(4)

    h_in = hprev[...]
    for g0 in range(0, ts, SUBLANES):
        if g0 == ts // 2:
            emit(5)
        a8 = a_scr[g0:g0 + SUBLANES, :]
        u8 = u_scr[g0:g0 + SUBLANES, :]
        for dist in (1, 2, 4):
            keep = row >= dist
            u_sh = pltpu.roll(u8, dist, axis=0)
            a_sh = pltpu.roll(a8, dist, axis=0)
            u8 = jnp.where(keep, a8 * u_sh + u8, u8)
            a8 = jnp.where(keep, a8 * a_sh, a8)
        h8 = a8 * h_in + u8
        h_scr[g0:g0 + SUBLANES, :] = h8
        h_in = h8[SUBLANES - 1:SUBLANES, :]
    hprev[...] = h_in
    emit(6)

    y_ref[...] = (h_scr[...] * _gelu_tanh(ug)).astype(y_ref.dtype)


def _block_diag(wb, group):
    n, c, _ = wb.shape
    wg = wb.reshape(n // group, group, c, c)
    eye = jnp.eye(group, dtype=wb.dtype)
    dense = wg[:, :, :, None, :] * eye[None, :, None, :, None]
    return dense.reshape(n // group, group * c, group * c)


def _in_lru(x2, g, w_bf, conv_w, conv_b, w_r, b_r, w_i, b_i, lam, later_weights, batch, ts):
    t, d = x2.shape
    nt = t // batch // ts
    steps = batch * nt
    assert all(w.shape[0] % steps == 0 for w in later_weights), "slabs must tile the weights"
    slabs = [pl.BlockSpec((w.shape[0] // steps, w.shape[1]), lambda b, i: (b * nt + i, 0))
             for w in later_weights]
    group = MXU_DIM // w_r.shape[1]
    wr = _block_diag(w_r, group).astype(BF16)
    wi = _block_diag(w_i, group).astype(BF16)
    tile = pl.BlockSpec((ts, d), lambda b, i: (b * nt + i, 0))
    row = _resident((1, d))
    dtypes = [BF16, BF16, BF16, F32, F32, BF16]
    outs = pl.pallas_call(
        _in_lru_kernel,
        grid=(batch, nt),
        in_specs=[tile, row, _resident(w_bf.shape), _resident(conv_w.shape), row,
                  _resident(wr.shape), row, _resident(wi.shape), row, row] + slabs,
        out_specs=[tile] * len(dtypes) + slabs,
        out_shape=([jax.ShapeDtypeStruct((t, d), dt) for dt in dtypes]
                   + [jax.ShapeDtypeStruct(w.shape, BF16) for w in later_weights]),
        scratch_shapes=[pltpu.VMEM((SUBLANES, d), F32), pltpu.VMEM((ts, d), F32),
                        pltpu.VMEM((ts, d), F32), pltpu.VMEM((ts, d), F32),
                        pltpu.VMEM((1, d), F32)],
        compiler_params=pltpu.CompilerParams(
            dimension_semantics=("arbitrary", "arbitrary"), vmem_limit_bytes=VMEM_LIMIT),
        name="in_lru",
    )(x2, g, w_bf, conv_w, conv_b.reshape(1, d), wr, b_r.reshape(1, d), wi, b_i.reshape(1, d),
      lam.reshape(1, d), *later_weights)
    return outs[:len(dtypes)], outs[len(dtypes):]


SOFTPLUS_CAP = 80.0


def _softplus(z):
    return jnp.maximum(z, jnp.log(1.0 + jnp.exp2(jnp.minimum(z, SOFTPLUS_CAP) * LOG2E)))


LANE_BLOCKS = 4


def _attn_kernel(qp_ref, qn_ref, k_ref, v_ref, tri_ref, o_ref, z_scr, att_scr, carry_scr):
    tq = qp_ref.shape[0]
    tk = tri_ref.shape[1]
    i = pl.program_id(2)
    nq = pl.num_programs(2) - 1
    lane = lax.broadcasted_iota(jnp.int32, (tq, ATTN_LANES), 1)
    tri = tri_ref[...]
    done = jnp.maximum(i - 1, 0)

    def stack_heads(q):
        q = q * (HEAD_DIM ** -0.5)
        zero = jnp.zeros_like(q)
        return jnp.concatenate(
            [jnp.where(lane // HEAD_DIM == h, q, zero) for h in range(HEADS_PER_BLOCK)], axis=0)

    def tile_rows(j):
        return pl.ds(pl.multiple_of(j * tk, tk), tk)

    def scores(q2, k_tiles):
        return lax.dot_general(q2, k_tiles, (((1,), (1,)), ((), ())), preferred_element_type=F32)

    def decay_terms(z):
        pos = _softplus(z)
        pos_bf = pos.astype(BF16)
        later = _bdot(pos_bf, tri)
        return z - pos, later, later[:, 0:1] + pos_bf[:, 0:1].astype(F32)

    def first_scores(q2, j, lanes):
        prev = jnp.maximum(j - 1, 0)
        return scores(q2, jnp.concatenate(
            [k_ref[tile_rows(j), lanes], k_ref[tile_rows(prev), lanes]], axis=0))

    @pl.when(i == 0)
    def _():
        for lb in range(LANE_BLOCKS):
            lanes = slice(lb * ATTN_LANES, (lb + 1) * ATTN_LANES)
            z_scr[lb] = first_scores(stack_heads(qp_ref[:, lanes]), 0, lanes)
        att_scr[...] = jnp.zeros(att_scr.shape, BF16)
        carry_scr[...] = jnp.full(carry_scr.shape, 2.0 * EXP_UNDERFLOW, F32)

    def first_steps(lb):
        lanes = slice(lb * ATTN_LANES, (lb + 1) * ATTN_LANES)

        carry_done = carry_scr[lb]
        v_prev = v_ref[tile_rows(jnp.maximum(done - 1, 0)), lanes]
        v_both = jnp.concatenate(
            [v_ref[tile_rows(done), lanes], jnp.where(done > 0, v_prev, jnp.zeros_like(v_prev))],
            axis=0)
        acc = _bdot(att_scr[lb], v_both)

        z = z_scr[lb]
        z_scr[lb] = first_scores(stack_heads(qn_ref[:, lanes]), jnp.minimum(i + 1, nq - 1),
                                 lanes)
        rq = lax.broadcasted_iota(jnp.int32, (HEADS_PER_BLOCK * tq, tk), 0) % tq
        ck = lax.broadcasted_iota(jnp.int32, (HEADS_PER_BLOCK * tq, tk), 1)
        z_diag = jnp.where(ck < rq, z[:, :tk], MASKED_LOGIT)
        z_prev = z[:, tk:]
        rows = HEADS_PER_BLOCK * tq
        log_beta, later, total = decay_terms(jnp.concatenate([z_diag, z_prev], axis=0))
        carry = total[:rows]
        att_scr[lb] = jnp.concatenate(
            [jnp.exp(log_beta[:rows] - later[:rows]),
             jnp.exp(log_beta[rows:] - (later[rows:] + carry))], axis=1).astype(BF16)
        carry_scr[lb] = carry + total[rows:]
        return acc, carry_done, jnp.min(carry_done)

    def earlier_tiles(lb, acc, carry_done, low):
        lanes = slice(lb * ATTN_LANES, (lb + 1) * ATTN_LANES)

        def cond(state):
            j, low, _, _ = state
            return jnp.logical_and(j >= 0, low < EXP_UNDERFLOW)

        def body(state):
            j, _, acc, carry = state
            z = scores(stack_heads(qp_ref[:, lanes]), k_ref[tile_rows(j), lanes])
            log_beta, later, total = decay_terms(z)
            att = jnp.exp(log_beta - (later + carry))
            acc = acc + _bdot(att.astype(BF16), v_ref[tile_rows(j), lanes])
            carry = carry + total
            return j - 1, jnp.min(carry), acc, carry

        _, _, acc, _ = lax.while_loop(cond, body, (i - 3, low, acc, carry_done))
        out = acc[:tq]
        for h in range(1, HEADS_PER_BLOCK):
            out = jnp.where(lane // HEAD_DIM == h, acc[h * tq:(h + 1) * tq], out)
        o_ref[:, lanes] = out.astype(o_ref.dtype)

    started = [first_steps(lb) for lb in range(LANE_BLOCKS)]
    for lb, (acc, carry_done, low) in enumerate(started):
        earlier_tiles(lb, acc, carry_done, low)


def _attn(q, k, v, batch, tq):
    t, width = q.shape
    seq = t // batch
    nq = seq // tq
    tk = tq
    group = LANE_BLOCKS * ATTN_LANES
    later = jnp.arange(tk)[:, None] > jnp.arange(tk)[None, :]

    def q_tile(shift):
        return pl.BlockSpec(
            (tq, group), lambda b, hp, i: (b * nq + jnp.clip(i + shift, 0, nq - 1), hp))

    kv_spec = pl.BlockSpec((seq, group), lambda b, hp, i: (b, hp))
    rows = HEADS_PER_BLOCK * tq
    return pl.pallas_call(
        _attn_kernel,
        grid=(batch, width // group, nq + 1),
        in_specs=[q_tile(-1), q_tile(1), kv_spec, kv_spec, _resident((tk, tk))],
        out_specs=q_tile(-1),
        out_shape=jax.ShapeDtypeStruct((t, width), BF16),
        scratch_shapes=[pltpu.VMEM((LANE_BLOCKS, rows, 2 * tk), F32),
                        pltpu.VMEM((LANE_BLOCKS, rows, 2 * tk), BF16),
                        pltpu.VMEM((LANE_BLOCKS, rows, 1), F32)],
        compiler_params=pltpu.CompilerParams(
            dimension_semantics=("arbitrary", "arbitrary", "arbitrary"),
            vmem_limit_bytes=VMEM_LIMIT),
        name="attn",
    )(q, q, k, v, later.astype(BF16))


def _post_kernel(x_ref, yl_ref, ya_ref, gl_ref, ga_ref, p_ref, wbl_ref, wba_ref, wo_ref,
                 g_mlp_ref, wu_ref, wd_ref, g_ple_ref, wpg_ref, wpl_ref, g_fin_ref, o_ref):
    merged = (_sigmoid(gl_ref[...]) * _bdot(yl_ref[...], wbl_ref[...])
              + _sigmoid(ga_ref[...]) * _bdot(ya_ref[...], wba_ref[...]))
    x = x_ref[...] + _bdot(merged.astype(BF16), wo_ref[...])
    h2 = _rmsnorm(x, g_mlp_ref[...]).astype(BF16)
    up = jnp.square(jnp.maximum(_bdot(h2, wu_ref[...]), 0.0))
    x = x + _bdot(up.astype(BF16), wd_ref[...])
    h3 = _rmsnorm(x, g_ple_ref[...]).astype(BF16)
    gate = _sigmoid(_bdot(h3, wpg_ref[...]))
    x = x + gate * _bdot(p_ref[...].astype(BF16), wpl_ref[...])
    o_ref[...] = _rmsnorm(x, g_fin_ref[...])


def _post(x2, y_lru, y_att, g_lru, g_att, p2, w_bl, w_ba, w_o, g_mlp, w_up, w_down, g_ple, w_pg,
          w_pl, g_fin, tm):
    t, d = x2.shape
    tile = pl.BlockSpec((tm, d), lambda i: (i, 0))
    ptile = pl.BlockSpec((tm, p2.shape[1]), lambda i: (i, 0))
    row = _resident((1, d))
    square = _resident((d, d))
    return pl.pallas_call(
        _post_kernel,
        grid=(t // tm,),
        in_specs=[tile] * 5 + [ptile, square, square, square, row, _resident(w_up.shape),
                               _resident(w_down.shape), row, square, _resident(w_pl.shape), row],
        out_specs=tile,
        out_shape=jax.ShapeDtypeStruct((t, d), F32),
        compiler_params=pltpu.CompilerParams(
            dimension_semantics=("arbitrary",), vmem_limit_bytes=VMEM_LIMIT),
        name="post",
    )(x2, y_lru, y_att, g_lru, g_att, p2, w_bl, w_ba, w_o, g_mlp, w_up, w_down, g_ple, w_pg,
      w_pl, g_fin)


def kernel(x, p, norm_mix_g, w_in, conv_w, conv_b, w_rgate, b_rgate, w_igate, b_igate, lru_lambda, w_br_lru, w_br_att, w_out, norm_mlp_g, w_mlp_up, w_mlp_down, norm_ple_g, w_ple_gate, w_ple, norm_final_g):
    batch, seq, d = x.shape
    depth = w_in.shape[0]
    t = batch * seq
    x2 = x.reshape(t, d)
    for l in range(depth):
        later = [w_br_lru[l], w_br_att[l], w_out[l], w_mlp_up[l], w_mlp_down[l], w_ple_gate[l]]
        (q, k, v, g_lru, g_att, y_lru), later_bf = _in_lru(
            x2, norm_mix_g[l].reshape(1, d), w_in[l].astype(BF16), conv_w[l], conv_b[l],
            w_rgate[l], b_rgate[l], w_igate[l], b_igate[l], lru_lambda[l], later, batch, ts=512)
        w_bl, w_ba, w_o, w_up, w_down, w_pg = later_bf
        y_att = _attn(q, k, v, batch, tq=256)
        assert l == depth - 1, "deeper stacks need a post variant without the final norm"
        x2 = _post(x2, y_lru, y_att, g_lru, g_att, p[l].reshape(t, -1), w_bl, w_ba, w_o,
                   norm_mlp_g[l].reshape(1, d), w_up, w_down, norm_ple_g[l].reshape(1, d),
                   w_pg, w_ple[l].astype(BF16), norm_final_g.reshape(1, d), tm=512)
    return x2.reshape(batch, seq, d)
```

```python
import jax
import jax.numpy as jnp
from jax import lax
from jax.experimental import pallas as pl
from jax.experimental.pallas import tpu as pltpu

F32 = jnp.float32
BF16 = jnp.bfloat16

NORM_EPS = 1e-6
LRU_C = 8.0
CONV_W = 4
HEAD_DIM = 64
SUBLANES = 8
MXU_DIM = 256
ATTN_LANES = 128
HEADS_PER_BLOCK = ATTN_LANES // HEAD_DIM

LOG2E = 1.4426950408889634
EXP_UNDERFLOW = 105.0
MASKED_LOGIT = -1e30

VMEM_LIMIT = 60 * 1024 * 1024


def _resident(shape):
    nd = len(shape)
    return pl.BlockSpec(shape, lambda *_: (0,) * nd, pipeline_mode=pl.Buffered(1))


def _rmsnorm(x, g):
    return x * lax.rsqrt(jnp.mean(x * x, axis=-1, keepdims=True) + NORM_EPS) * g


def _sigmoid(x):
    return 0.5 * jnp.tanh(0.5 * x) + 0.5


def _bdot(a, b):
    return jnp.dot(a, b, preferred_element_type=F32)


GELU_K = (2.0 / 3.141592653589793) ** 0.5


def _twice_gelu_tanh(x):
    inner = x * ((GELU_K * 0.044715) * (x * x) + GELU_K)
    return x * (jnp.tanh(inner) + 1.0)


def _shift_rows(padded, back):
    return pltpu.roll(padded, back, axis=0)[SUBLANES:]


def _in_lru_kernel(x_ref, g_ref, w_ref, cw_ref, cb_ref, wr_ref, br_ref, wi_ref, bi_ref, lam_ref,
                   *rest):
    n_cast = (len(rest) - 11) // 2
    cast_srcs, rest = rest[:n_cast], rest[n_cast:]
    q_ref, k_ref, v_ref, gl_ref, ga_ref, y_ref = rest[:6]
    cast_dsts, (tail, a_scr, u_scr, h_scr, hprev) = rest[6:6 + n_cast], rest[6 + n_cast:]
    ts, d = x_ref.shape
    for src, dst in zip(cast_srcs, cast_dsts):
        dst[...] = src[...].astype(dst.dtype)

    @pl.when(pl.program_id(1) == 0)
    def _():
        tail[...] = jnp.zeros(tail.shape, F32)
        hprev[...] = jnp.zeros(hprev.shape, F32)

    h = _rmsnorm(x_ref[...], g_ref[...]).astype(BF16)

    def proj(n):
        return _bdot(h, w_ref[:, n * d:(n + 1) * d])

    outs = {2: q_ref, 3: k_ref, 4: v_ref, 5: gl_ref, 6: ga_ref}

    def emit(n):
        outs[n][...] = proj(n).astype(outs[n].dtype)

    ux = proj(0)
    ug = proj(1)
    row = lax.broadcasted_iota(jnp.int32, (SUBLANES, d), 0)
    padded = jnp.concatenate([tail[...], ux], axis=0)
    c = cb_ref[...]
    for k in range(CONV_W):
        back = CONV_W - 1 - k
        xs = ux if back == 0 else _shift_rows(padded, back)
        c = c + cw_ref[k:k + 1, :] * xs
    tail[...] = ux[ts - SUBLANES:, :]
    emit(2)

    cb = c.astype(BF16)
    gr, gi = [], []
    for j in range(d // MXU_DIM):
        sl = slice(j * MXU_DIM, (j + 1) * MXU_DIM)
        gr.append(_bdot(cb[:, sl], wr_ref[j]))
        gi.append(_bdot(cb[:, sl], wi_ref[j]))
    twice_ig = jnp.tanh(jnp.concatenate(gi, axis=1) + 0.5 * bi_ref[...]) + 1.0
    emit(3)

    neg_lam = -lam_ref[...]
    softplus = jnp.maximum(neg_lam, 0.0) + jnp.log1p(jnp.exp(-jnp.abs(neg_lam)))
    half_scale = (0.5 * LRU_C) * softplus
    neg_log_a = (half_scale * jnp.tanh(jnp.concatenate(gr, axis=1) + 0.5 * br_ref[...])
                 + half_scale)
    a = jnp.exp2(neg_log_a * (-LOG2E))
    one_m_a2 = jnp.tanh(neg_log_a) * (a * a + 1.0)
    mult = jnp.where(one_m_a2 > 0.0, one_m_a2 * lax.rsqrt(one_m_a2), 0.0)
    a_scr[...] = a
    u_scr[...] = mult * (twice_ig * c)
    emit(4)

    h_in = hprev[...]
    for g0 in range(0, ts, SUBLANES):
        if g0 == ts // 2:
            emit(5)
        a8 = a_scr[g0:g0 + SUBLANES, :]
        u8 = u_scr[g0:g0 + SUBLANES, :]
        for dist in (1, 2, 4):
            keep = row >= dist
            u_sh = pltpu.roll(u8, dist, axis=0)
            a_sh = pltpu.roll(a8, dist, axis=0)
            u8 = jnp.where(keep, a8 * u_sh + u8, u8)
            a8 = jnp.where(keep, a8 * a_sh, a8)
        h8 = a8 * h_in + u8
        h_scr[g0:g0 + SUBLANES, :] = h8
        h_in = h8[SUBLANES - 1:SUBLANES, :]
    hprev[...] = h_in
    emit(6)

    y_ref[...] = (h_scr[...] * _twice_gelu_tanh(ug)).astype(y_ref.dtype)


def _block_diag(wb, group):
    n, c, _ = wb.shape
    wg = wb.reshape(n // group, group, c, c)
    eye = jnp.eye(group, dtype=wb.dtype)
    dense = wg[:, :, :, None, :] * eye[None, :, None, :, None]
    return dense.reshape(n // group, group * c, group * c)


def _in_lru(x2, g, w_bf, conv_w, conv_b, w_r, b_r, w_i, b_i, lam, later_weights, batch, ts):
    t, d = x2.shape
    nt = t // batch // ts
    steps = batch * nt
    assert all(w.shape[0] % steps == 0 for w in later_weights), "slabs must tile the weights"
    slabs = [pl.BlockSpec((w.shape[0] // steps, w.shape[1]), lambda b, i: (b * nt + i, 0))
             for w in later_weights]
    group = MXU_DIM // w_r.shape[1]
    conv_w, conv_b = 0.25 * conv_w, 0.25 * conv_b
    wr = (2.0 * _block_diag(w_r, group)).astype(BF16)
    wi = (2.0 * _block_diag(w_i, group)).astype(BF16)
    tile = pl.BlockSpec((ts, d), lambda b, i: (b * nt + i, 0))
    row = _resident((1, d))
    dtypes = [BF16, BF16, BF16, F32, F32, BF16]
    outs = pl.pallas_call(
        _in_lru_kernel,
        grid=(batch, nt),
        in_specs=[tile, row, _resident(w_bf.shape), _resident(conv_w.shape), row,
                  _resident(wr.shape), row, _resident(wi.shape), row, row] + slabs,
        out_specs=[tile] * len(dtypes) + slabs,
        out_shape=([jax.ShapeDtypeStruct((t, d), dt) for dt in dtypes]
                   + [jax.ShapeDtypeStruct(w.shape, BF16) for w in later_weights]),
        scratch_shapes=[pltpu.VMEM((SUBLANES, d), F32), pltpu.VMEM((ts, d), F32),
                        pltpu.VMEM((ts, d), F32), pltpu.VMEM((ts, d), F32),
                        pltpu.VMEM((1, d), F32)],
        compiler_params=pltpu.CompilerParams(
            dimension_semantics=("arbitrary", "arbitrary"), vmem_limit_bytes=VMEM_LIMIT),
        name="in_lru",
    )(x2, g, w_bf, conv_w, conv_b.reshape(1, d), wr, b_r.reshape(1, d), wi, b_i.reshape(1, d),
      lam.reshape(1, d), *later_weights)
    return outs[:len(dtypes)], outs[len(dtypes):]


SOFTPLUS_CAP = 80.0


def _softplus(z):
    return jnp.maximum(z, jnp.log(1.0 + jnp.exp2(jnp.minimum(z, SOFTPLUS_CAP) * LOG2E)))


LANE_BLOCKS = 4


def _attn_kernel(qp_ref, qn_ref, k_ref, v_ref, tri_ref, o_ref, z_scr, att_scr, carry_scr):
    tq = qp_ref.shape[0]
    tk = tri_ref.shape[1]
    i = pl.program_id(2)
    nq = pl.num_programs(2) - 1
    lane = lax.broadcasted_iota(jnp.int32, (tq, ATTN_LANES), 1)
    tri = tri_ref[...]
    done = jnp.maximum(i - 1, 0)

    def stack_heads(q):
        q = q * (HEAD_DIM ** -0.5)
        zero = jnp.zeros_like(q)
        return jnp.concatenate(
            [jnp.where(lane // HEAD_DIM == h, q, zero) for h in range(HEADS_PER_BLOCK)], axis=0)

    def tile_rows(j):
        return pl.ds(pl.multiple_of(j * tk, tk), tk)

    def scores(q2, k_tiles):
        return lax.dot_general(q2, k_tiles, (((1,), (1,)), ((), ())), preferred_element_type=F32)

    def decay_terms(z):
        pos = _softplus(z)
        pos_bf = pos.astype(BF16)
        later = _bdot(pos_bf, tri)
        return z - pos, later, later[:, 0:1] + pos_bf[:, 0:1].astype(F32)

    def first_scores(q2, j, lanes):
        prev = jnp.maximum(j - 1, 0)
        return scores(q2, jnp.concatenate(
            [k_ref[tile_rows(j), lanes], k_ref[tile_rows(prev), lanes]], axis=0))

    @pl.when(i == 0)
    def _():
        for lb in range(LANE_BLOCKS):
            lanes = slice(lb * ATTN_LANES, (lb + 1) * ATTN_LANES)
            z_scr[lb] = first_scores(stack_heads(qp_ref[:, lanes]), 0, lanes)
        att_scr[...] = jnp.zeros(att_scr.shape, BF16)
        carry_scr[...] = jnp.full(carry_scr.shape, 2.0 * EXP_UNDERFLOW, F32)

    def first_steps(lb):
        lanes = slice(lb * ATTN_LANES, (lb + 1) * ATTN_LANES)

        carry_done = carry_scr[lb]
        v_prev = v_ref[tile_rows(jnp.maximum(done - 1, 0)), lanes]
        v_both = jnp.concatenate(
            [v_ref[tile_rows(done), lanes], jnp.where(done > 0, v_prev, jnp.zeros_like(v_prev))],
            axis=0)
        acc = _bdot(att_scr[lb], v_both)

        z = z_scr[lb]
        z_scr[lb] = first_scores(stack_heads(qn_ref[:, lanes]), jnp.minimum(i + 1, nq - 1),
                                 lanes)
        rq = lax.broadcasted_iota(jnp.int32, (HEADS_PER_BLOCK * tq, tk), 0) % tq
        ck = lax.broadcasted_iota(jnp.int32, (HEADS_PER_BLOCK * tq, tk), 1)
        z_diag = jnp.where(ck < rq, z[:, :tk], MASKED_LOGIT)
        z_prev = z[:, tk:]
        rows = HEADS_PER_BLOCK * tq
        log_beta, later, total = decay_terms(jnp.concatenate([z_diag, z_prev], axis=0))
        carry = total[:rows]
        att_scr[lb] = jnp.concatenate(
            [jnp.exp(log_beta[:rows] - later[:rows]),
             jnp.exp(log_beta[rows:] - (later[rows:] + carry))], axis=1).astype(BF16)
        carry_scr[lb] = carry + total[rows:]
        return acc, carry_done, jnp.min(carry_done)

    def earlier_tiles(lb, acc, carry_done, low):
        lanes = slice(lb * ATTN_LANES, (lb + 1) * ATTN_LANES)

        def cond(state):
            j, low, _, _ = state
            return jnp.logical_and(j >= 0, low < EXP_UNDERFLOW)

        def body(state):
            j, _, acc, carry = state
            z = scores(stack_heads(qp_ref[:, lanes]), k_ref[tile_rows(j), lanes])
            log_beta, later, total = decay_terms(z)
            att = jnp.exp(log_beta - (later + carry))
            acc = acc + _bdot(att.astype(BF16), v_ref[tile_rows(j), lanes])
            carry = carry + total
            return j - 1, jnp.min(carry), acc, carry

        _, _, acc, _ = lax.while_loop(cond, body, (i - 3, low, acc, carry_done))
        out = acc[:tq]
        for h in range(1, HEADS_PER_BLOCK):
            out = jnp.where(lane // HEAD_DIM == h, acc[h * tq:(h + 1) * tq], out)
        o_ref[:, lanes] = out.astype(o_ref.dtype)

    started = [first_steps(lb) for lb in range(LANE_BLOCKS)]
    for lb, (acc, carry_done, low) in enumerate(started):
        earlier_tiles(lb, acc, carry_done, low)


def _attn(q, k, v, batch, tq):
    t, width = q.shape
    seq = t // batch
    nq = seq // tq
    tk = tq
    group = LANE_BLOCKS * ATTN_LANES
    later = jnp.arange(tk)[:, None] > jnp.arange(tk)[None, :]

    def q_tile(shift):
        return pl.BlockSpec(
            (tq, group), lambda b, hp, i: (b * nq + jnp.clip(i + shift, 0, nq - 1), hp))

    kv_spec = pl.BlockSpec((seq, group), lambda b, hp, i: (b, hp))
    rows = HEADS_PER_BLOCK * tq
    return pl.pallas_call(
        _attn_kernel,
        grid=(batch, width // group, nq + 1),
        in_specs=[q_tile(-1), q_tile(1), kv_spec, kv_spec, _resident((tk, tk))],
        out_specs=q_tile(-1),
        out_shape=jax.ShapeDtypeStruct((t, width), BF16),
        scratch_shapes=[pltpu.VMEM((LANE_BLOCKS, rows, 2 * tk), F32),
                        pltpu.VMEM((LANE_BLOCKS, rows, 2 * tk), BF16),
                        pltpu.VMEM((LANE_BLOCKS, rows, 1), F32)],
        compiler_params=pltpu.CompilerParams(
            dimension_semantics=("arbitrary", "arbitrary", "arbitrary"),
            vmem_limit_bytes=VMEM_LIMIT),
        name="attn",
    )(q, q, k, v, later.astype(BF16))


def _post_kernel(x_ref, yl_ref, ya_ref, gl_ref, ga_ref, p_ref, wbl_ref, wba_ref, wo_ref,
                 g_mlp_ref, wu_ref, wd_ref, g_ple_ref, wpg_ref, wpl_ref, g_fin_ref, o_ref):
    merged = (_sigmoid(gl_ref[...]) * _bdot(yl_ref[...], wbl_ref[...])
              + _sigmoid(ga_ref[...]) * _bdot(ya_ref[...], wba_ref[...]))
    x = x_ref[...] + _bdot(merged.astype(BF16), wo_ref[...])
    h2 = _rmsnorm(x, g_mlp_ref[...]).astype(BF16)
    up = jnp.square(jnp.maximum(_bdot(h2, wu_ref[...]), 0.0))
    x = x + _bdot(up.astype(BF16), wd_ref[...])
    h3 = _rmsnorm(x, g_ple_ref[...]).astype(BF16)
    gate = _sigmoid(_bdot(h3, wpg_ref[...]))
    x = x + gate * _bdot(p_ref[...].astype(BF16), wpl_ref[...])
    o_ref[...] = _rmsnorm(x, g_fin_ref[...])


def _post(x2, y_lru, y_att, g_lru, g_att, p2, w_bl, w_ba, w_o, g_mlp, w_up, w_down, g_ple, w_pg,
          w_pl, g_fin, tm):
    t, d = x2.shape
    tile = pl.BlockSpec((tm, d), lambda i: (i, 0))
    ptile = pl.BlockSpec((tm, p2.shape[1]), lambda i: (i, 0))
    row = _resident((1, d))
    square = _resident((d, d))
    return pl.pallas_call(
        _post_kernel,
        grid=(t // tm,),
        in_specs=[tile] * 5 + [ptile, square, square, square, row, _resident(w_up.shape),
                               _resident(w_down.shape), row, square, _resident(w_pl.shape), row],
        out_specs=tile,
        out_shape=jax.ShapeDtypeStruct((t, d), F32),
        compiler_params=pltpu.CompilerParams(
            dimension_semantics=("arbitrary",), vmem_limit_bytes=VMEM_LIMIT),
        name="post",
    )(x2, y_lru, y_att, g_lru, g_att, p2, w_bl, w_ba, w_o, g_mlp, w_up, w_down, g_ple, w_pg,
      w_pl, g_fin)


def kernel(x, p, norm_mix_g, w_in, conv_w, conv_b, w_rgate, b_rgate, w_igate, b_igate, lru_lambda, w_br_lru, w_br_att, w_out, norm_mlp_g, w_mlp_up, w_mlp_down, norm_ple_g, w_ple_gate, w_ple, norm_final_g):
    batch, seq, d = x.shape
    depth = w_in.shape[0]
    t = batch * seq
    x2 = x.reshape(t, d)
    for l in range(depth):
        later = [w_br_lru[l], w_br_att[l], w_out[l], w_mlp_up[l], w_mlp_down[l], w_ple_gate[l]]
        (q, k, v, g_lru, g_att, y_lru), later_bf = _in_lru(
            x2, norm_mix_g[l].reshape(1, d), w_in[l].astype(BF16), conv_w[l], conv_b[l],
            w_rgate[l], b_rgate[l], w_igate[l], b_igate[l], lru_lambda[l], later, batch, ts=512)
        w_bl, w_ba, w_o, w_up, w_down, w_pg = later_bf
        y_att = _attn(q, k, v, batch, tq=256)
        assert l == depth - 1, "deeper stacks need a post variant without the final norm"
        x2 = _post(x2, y_lru, y_att, g_lru, g_att, p[l].reshape(t, -1), w_bl, w_ba, w_o,
                   norm_mlp_g[l].reshape(1, d), w_up, w_down, norm_ple_g[l].reshape(1, d),
                   w_pg, w_ple[l].astype(BF16), norm_final_g.reshape(1, d), tm=512)
    return x2.reshape(batch, seq, d)
```

```python
import jax
import jax.numpy as jnp
from jax import lax
from jax.experimental import pallas as pl
from jax.experimental.pallas import tpu as pltpu

F32 = jnp.float32
BF16 = jnp.bfloat16

NORM_EPS = 1e-6
LRU_C = 8.0
CONV_W = 4
HEAD_DIM = 64
SUBLANES = 8
MXU_DIM = 256
ATTN_LANES = 128
HEADS_PER_BLOCK = ATTN_LANES // HEAD_DIM

LOG2E = 1.4426950408889634
EXP_UNDERFLOW = 105.0
MASKED_LOGIT = -1e30

VMEM_LIMIT = 60 * 1024 * 1024


def _resident(shape):
    nd = len(shape)
    return pl.BlockSpec(shape, lambda *_: (0,) * nd, pipeline_mode=pl.Buffered(1))


def _rmsnorm(x, g):
    return x * lax.rsqrt(jnp.mean(x * x, axis=-1, keepdims=True) + NORM_EPS) * g


def _sigmoid(x):
    return 0.5 * jnp.tanh(0.5 * x) + 0.5


def _bdot(a, b):
    return jnp.dot(a, b, preferred_element_type=F32)


GELU_K = (2.0 / 3.141592653589793) ** 0.5


def _twice_gelu_tanh(x):
    inner = x * ((GELU_K * 0.044715) * (x * x) + GELU_K)
    return x * (jnp.tanh(inner) + 1.0)


def _shift_rows(padded, back):
    return pltpu.roll(padded, back, axis=0)[SUBLANES:]


def _in_lru_kernel(x_ref, g_ref, w_ref, cw_ref, cb_ref, wr_ref, br_ref, wi_ref, bi_ref, lam_ref,
                   *rest):
    n_cast = (len(rest) - 11) // 2
    cast_srcs, rest = rest[:n_cast], rest[n_cast:]
    q_ref, k_ref, v_ref, gl_ref, ga_ref, y_ref = rest[:6]
    cast_dsts, (tail, a_scr, u_scr, h_scr, hprev) = rest[6:6 + n_cast], rest[6 + n_cast:]
    ts, d = x_ref.shape
    for src, dst in zip(cast_srcs, cast_dsts):
        dst[...] = src[...].astype(dst.dtype)

    @pl.when(pl.program_id(1) == 0)
    def _():
        tail[...] = jnp.zeros(tail.shape, F32)
        hprev[...] = jnp.zeros(hprev.shape, F32)

    h = _rmsnorm(x_ref[...], g_ref[...]).astype(BF16)

    def proj(n):
        return _bdot(h, w_ref[:, n * d:(n + 1) * d])

    outs = {2: q_ref, 3: k_ref, 4: v_ref, 5: gl_ref, 6: ga_ref}

    def emit(n):
        outs[n][...] = proj(n).astype(outs[n].dtype)

    ux = proj(0)
    ug = proj(1)
    row = lax.broadcasted_iota(jnp.int32, (SUBLANES, d), 0)
    padded = jnp.concatenate([tail[...], ux], axis=0)
    c = cb_ref[...]
    for k in range(CONV_W):
        back = CONV_W - 1 - k
        xs = ux if back == 0 else _shift_rows(padded, back)
        c = c + cw_ref[k:k + 1, :] * xs
    tail[...] = ux[ts - SUBLANES:, :]
    emit(2)

    cb = c.astype(BF16)
    gr, gi = [], []
    for j in range(d // MXU_DIM):
        sl = slice(j * MXU_DIM, (j + 1) * MXU_DIM)
        gr.append(_bdot(cb[:, sl], wr_ref[j]))
        gi.append(_bdot(cb[:, sl], wi_ref[j]))
    twice_ig = jnp.tanh(jnp.concatenate(gi, axis=1) + 0.5 * bi_ref[...]) + 1.0
    emit(3)

    neg_lam = -lam_ref[...]
    softplus = jnp.maximum(neg_lam, 0.0) + jnp.log1p(jnp.exp(-jnp.abs(neg_lam)))
    half_scale = (0.5 * LRU_C) * softplus
    neg_log_a = (half_scale * jnp.tanh(jnp.concatenate(gr, axis=1) + 0.5 * br_ref[...])
                 + half_scale)
    a = jnp.exp2(neg_log_a * (-LOG2E))
    one_m_a2 = jnp.tanh(neg_log_a) * (a * a + 1.0)
    mult = jnp.where(one_m_a2 > 0.0, one_m_a2 * lax.rsqrt(one_m_a2), 0.0)
    a_scr[...] = a
    u_scr[...] = mult * (twice_ig * c)
    emit(4)

    h_in = hprev[...]
    half = SUBLANES // 2
    keeps = [(dist, row % half >= dist) for dist in (1, 2)]
    lower = row < half
    for g0 in range(0, ts, SUBLANES):
        if g0 == ts // 2:
            emit(5)
        a8 = a_scr[g0:g0 + SUBLANES, :]
        u8 = u_scr[g0:g0 + SUBLANES, :]
        for dist, keep in keeps:
            u_sh = pltpu.roll(u8, dist, axis=0)
            a_sh = pltpu.roll(a8, dist, axis=0)
            u8 = jnp.where(keep, a8 * u_sh + u8, u8)
            a8 = jnp.where(keep, a8 * a_sh, a8)
        h_lower = a8 * h_in + u8
        h_upper = a8 * h_lower[half - 1:half, :] + u8
        h8 = jnp.where(lower, h_lower, h_upper)
        h_scr[g0:g0 + SUBLANES, :] = h8
        h_in = h8[SUBLANES - 1:SUBLANES, :]
    hprev[...] = h_in
    emit(6)

    y_ref[...] = (h_scr[...] * _twice_gelu_tanh(ug)).astype(y_ref.dtype)


def _block_diag(wb, group):
    n, c, _ = wb.shape
    wg = wb.reshape(n // group, group, c, c)
    eye = jnp.eye(group, dtype=wb.dtype)
    dense = wg[:, :, :, None, :] * eye[None, :, None, :, None]
    return dense.reshape(n // group, group * c, group * c)


def _in_lru(x2, g, w_bf, conv_w, conv_b, w_r, b_r, w_i, b_i, lam, later_weights, batch, ts):
    t, d = x2.shape
    nt = t // batch // ts
    steps = batch * nt
    assert all(w.shape[0] % steps == 0 for w in later_weights), "slabs must tile the weights"
    slabs = [pl.BlockSpec((w.shape[0] // steps, w.shape[1]), lambda b, i: (b * nt + i, 0))
             for w in later_weights]
    group = MXU_DIM // w_r.shape[1]
    conv_w, conv_b = 0.25 * conv_w, 0.25 * conv_b
    wr = (2.0 * _block_diag(w_r, group)).astype(BF16)
    wi = (2.0 * _block_diag(w_i, group)).astype(BF16)
    tile = pl.BlockSpec((ts, d), lambda b, i: (b * nt + i, 0))
    row = _resident((1, d))
    dtypes = [BF16, BF16, BF16, F32, F32, BF16]
    outs = pl.pallas_call(
        _in_lru_kernel,
        grid=(batch, nt),
        in_specs=[tile, row, _resident(w_bf.shape), _resident(conv_w.shape), row,
                  _resident(wr.shape), row, _resident(wi.shape), row, row] + slabs,
        out_specs=[tile] * len(dtypes) + slabs,
        out_shape=([jax.ShapeDtypeStruct((t, d), dt) for dt in dtypes]
                   + [jax.ShapeDtypeStruct(w.shape, BF16) for w in later_weights]),
        scratch_shapes=[pltpu.VMEM((SUBLANES, d), F32), pltpu.VMEM((ts, d), F32),
                        pltpu.VMEM((ts, d), F32), pltpu.VMEM((ts, d), F32),
                        pltpu.VMEM((1, d), F32)],
        compiler_params=pltpu.CompilerParams(
            dimension_semantics=("arbitrary", "arbitrary"), vmem_limit_bytes=VMEM_LIMIT),
        name="in_lru",
    )(x2, g, w_bf, conv_w, conv_b.reshape(1, d), wr, b_r.reshape(1, d), wi, b_i.reshape(1, d),
      lam.reshape(1, d), *later_weights)
    return outs[:len(dtypes)], outs[len(dtypes):]


SOFTPLUS_CAP = 80.0


def _softplus(z):
    return jnp.maximum(z, jnp.log(1.0 + jnp.exp2(jnp.minimum(z, SOFTPLUS_CAP) * LOG2E)))


LANE_BLOCKS = 4


def _attn_kernel(qp_ref, qn_ref, k_ref, v_ref, tri_ref, o_ref, z_scr, att_scr, carry_scr):
    tq = qp_ref.shape[0]
    tk = tri_ref.shape[1]
    i = pl.program_id(2)
    nq = pl.num_programs(2) - 1
    lane = lax.broadcasted_iota(jnp.int32, (tq, ATTN_LANES), 1)
    tri = tri_ref[...]
    done = jnp.maximum(i - 1, 0)

    def stack_heads(q):
        q = q * (HEAD_DIM ** -0.5)
        zero = jnp.zeros_like(q)
        return jnp.concatenate(
            [jnp.where(lane // HEAD_DIM == h, q, zero) for h in range(HEADS_PER_BLOCK)], axis=0)

    def tile_rows(j):
        return pl.ds(pl.multiple_of(j * tk, tk), tk)

    def scores(q2, k_tiles):
        return lax.dot_general(q2, k_tiles, (((1,), (1,)), ((), ())), preferred_element_type=F32)

    def decay_terms(z):
        pos = _softplus(z)
        pos_bf = pos.astype(BF16)
        later = _bdot(pos_bf, tri)
        return z - pos, later, later[:, 0:1] + pos_bf[:, 0:1].astype(F32)

    def first_scores(q2, j, lanes):
        prev = jnp.maximum(j - 1, 0)
        return scores(q2, jnp.concatenate(
            [k_ref[tile_rows(j), lanes], k_ref[tile_rows(prev), lanes]], axis=0))

    @pl.when(i == 0)
    def _():
        for lb in range(LANE_BLOCKS):
            lanes = slice(lb * ATTN_LANES, (lb + 1) * ATTN_LANES)
            z_scr[lb] = first_scores(stack_heads(qp_ref[:, lanes]), 0, lanes)
        att_scr[...] = jnp.zeros(att_scr.shape, BF16)
        carry_scr[...] = jnp.full(carry_scr.shape, 2.0 * EXP_UNDERFLOW, F32)

    def first_steps(lb):
        lanes = slice(lb * ATTN_LANES, (lb + 1) * ATTN_LANES)

        carry_done = carry_scr[lb]
        v_prev = v_ref[tile_rows(jnp.maximum(done - 1, 0)), lanes]
        v_both = jnp.concatenate(
            [v_ref[tile_rows(done), lanes], jnp.where(done > 0, v_prev, jnp.zeros_like(v_prev))],
            axis=0)
        acc = _bdot(att_scr[lb], v_both)

        z = z_scr[lb]
        z_scr[lb] = first_scores(stack_heads(qn_ref[:, lanes]), jnp.minimum(i + 1, nq - 1),
                                 lanes)
        rq = lax.broadcasted_iota(jnp.int32, (HEADS_PER_BLOCK * tq, tk), 0) % tq
        ck = lax.broadcasted_iota(jnp.int32, (HEADS_PER_BLOCK * tq, tk), 1)
        z_diag = jnp.where(ck < rq, z[:, :tk], MASKED_LOGIT)
        z_prev = z[:, tk:]
        rows = HEADS_PER_BLOCK * tq
        log_beta, later, total = decay_terms(jnp.concatenate([z_diag, z_prev], axis=0))
        carry = total[:rows]
        att_scr[lb] = jnp.concatenate(
            [jnp.exp(log_beta[:rows] - later[:rows]),
             jnp.exp(log_beta[rows:] - (later[rows:] + carry))], axis=1).astype(BF16)
        carry_scr[lb] = carry + total[rows:]
        return acc, carry_done, jnp.min(carry_done)

    def earlier_tiles(lb, acc, carry_done, low):
        lanes = slice(lb * ATTN_LANES, (lb + 1) * ATTN_LANES)

        def cond(state):
            j, low, _, _ = state
            return jnp.logical_and(j >= 0, low < EXP_UNDERFLOW)

        def body(state):
            j, _, acc, carry = state
            z = scores(stack_heads(qp_ref[:, lanes]), k_ref[tile_rows(j), lanes])
            log_beta, later, total = decay_terms(z)
            att = jnp.exp(log_beta - (later + carry))
            acc = acc + _bdot(att.astype(BF16), v_ref[tile_rows(j), lanes])
            carry = carry + total
            return j - 1, jnp.min(carry), acc, carry

        _, _, acc, _ = lax.while_loop(cond, body, (i - 3, low, acc, carry_done))
        out = acc[:tq]
        for h in range(1, HEADS_PER_BLOCK):
            out = jnp.where(lane // HEAD_DIM == h, acc[h * tq:(h + 1) * tq], out)
        o_ref[:, lanes] = out.astype(o_ref.dtype)

    started = [first_steps(lb) for lb in range(LANE_BLOCKS)]
    for lb, (acc, carry_done, low) in enumerate(started):
        earlier_tiles(lb, acc, carry_done, low)


def _attn(q, k, v, batch, tq):
    t, width = q.shape
    seq = t // batch
    nq = seq // tq
    tk = tq
    group = LANE_BLOCKS * ATTN_LANES
    later = jnp.arange(tk)[:, None] > jnp.arange(tk)[None, :]

    def q_tile(shift):
        return pl.BlockSpec(
            (tq, group), lambda b, hp, i: (b * nq + jnp.clip(i + shift, 0, nq - 1), hp))

    kv_spec = pl.BlockSpec((seq, group), lambda b, hp, i: (b, hp))
    rows = HEADS_PER_BLOCK * tq
    return pl.pallas_call(
        _attn_kernel,
        grid=(batch, width // group, nq + 1),
        in_specs=[q_tile(-1), q_tile(1), kv_spec, kv_spec, _resident((tk, tk))],
        out_specs=q_tile(-1),
        out_shape=jax.ShapeDtypeStruct((t, width), BF16),
        scratch_shapes=[pltpu.VMEM((LANE_BLOCKS, rows, 2 * tk), F32),
                        pltpu.VMEM((LANE_BLOCKS, rows, 2 * tk), BF16),
                        pltpu.VMEM((LANE_BLOCKS, rows, 1), F32)],
        compiler_params=pltpu.CompilerParams(
            dimension_semantics=("arbitrary", "arbitrary", "arbitrary"),
            vmem_limit_bytes=VMEM_LIMIT),
        name="attn",
    )(q, q, k, v, later.astype(BF16))


def _post_kernel(x_ref, yl_ref, ya_ref, gl_ref, ga_ref, p_ref, wbl_ref, wba_ref, wo_ref,
                 g_mlp_ref, wu_ref, wd_ref, g_ple_ref, wpg_ref, wpl_ref, g_fin_ref, o_ref):
    merged = (_sigmoid(gl_ref[...]) * _bdot(yl_ref[...], wbl_ref[...])
              + _sigmoid(ga_ref[...]) * _bdot(ya_ref[...], wba_ref[...]))
    x = x_ref[...] + _bdot(merged.astype(BF16), wo_ref[...])
    h2 = _rmsnorm(x, g_mlp_ref[...]).astype(BF16)
    up = jnp.square(jnp.maximum(_bdot(h2, wu_ref[...]), 0.0))
    x = x + _bdot(up.astype(BF16), wd_ref[...])
    h3 = _rmsnorm(x, g_ple_ref[...]).astype(BF16)
    gate = _sigmoid(_bdot(h3, wpg_ref[...]))
    x = x + gate * _bdot(p_ref[...].astype(BF16), wpl_ref[...])
    o_ref[...] = _rmsnorm(x, g_fin_ref[...])


def _post(x2, y_lru, y_att, g_lru, g_att, p2, w_bl, w_ba, w_o, g_mlp, w_up, w_down, g_ple, w_pg,
          w_pl, g_fin, tm):
    t, d = x2.shape
    tile = pl.BlockSpec((tm, d), lambda i: (i, 0))
    ptile = pl.BlockSpec((tm, p2.shape[1]), lambda i: (i, 0))
    row = _resident((1, d))
    square = _resident((d, d))
    return pl.pallas_call(
        _post_kernel,
        grid=(t // tm,),
        in_specs=[tile] * 5 + [ptile, square, square, square, row, _resident(w_up.shape),
                               _resident(w_down.shape), row, square, _resident(w_pl.shape), row],
        out_specs=tile,
        out_shape=jax.ShapeDtypeStruct((t, d), F32),
        compiler_params=pltpu.CompilerParams(
            dimension_semantics=("arbitrary",), vmem_limit_bytes=VMEM_LIMIT),
        name="post",
    )(x2, y_lru, y_att, g_lru, g_att, p2, w_bl, w_ba, w_o, g_mlp, w_up, w_down, g_ple, w_pg,
      w_pl, g_fin)


def kernel(x, p, norm_mix_g, w_in, conv_w, conv_b, w_rgate, b_rgate, w_igate, b_igate, lru_lambda, w_br_lru, w_br_att, w_out, norm_mlp_g, w_mlp_up, w_mlp_down, norm_ple_g, w_ple_gate, w_ple, norm_final_g):
    batch, seq, d = x.shape
    depth = w_in.shape[0]
    t = batch * seq
    x2 = x.reshape(t, d)
    for l in range(depth):
        later = [w_br_lru[l], w_br_att[l], w_out[l], w_mlp_up[l], w_mlp_down[l], w_ple_gate[l]]
        (q, k, v, g_lru, g_att, y_lru), later_bf = _in_lru(
            x2, norm_mix_g[l].reshape(1, d), w_in[l].astype(BF16), conv_w[l], conv_b[l],
            w_rgate[l], b_rgate[l], w_igate[l], b_igate[l], lru_lambda[l], later, batch, ts=512)
        w_bl, w_ba, w_o, w_up, w_down, w_pg = later_bf
        y_att = _attn(q, k, v, batch, tq=256)
        assert l == depth - 1, "deeper stacks need a post variant without the final norm"
        x2 = _post(x2, y_lru, y_att, g_lru, g_att, p[l].reshape(t, -1), w_bl, w_ba, w_o,
                   norm_mlp_g[l].reshape(1, d), w_up, w_down, norm_ple_g[l].reshape(1, d),
                   w_pg, w_ple[l].astype(BF16), norm_final_g.reshape(1, d), tm=512)
    return x2.reshape(batch, seq, d)
```

```python
import jax
import jax.numpy as jnp
from jax import lax
from jax.experimental import pallas as pl
from jax.experimental.pallas import tpu as pltpu

F32 = jnp.float32
BF16 = jnp.bfloat16

NORM_EPS = 1e-6
LRU_C = 8.0
CONV_W = 4
HEAD_DIM = 64
SUBLANES = 8
MXU_DIM = 256
ATTN_LANES = 128
HEADS_PER_BLOCK = ATTN_LANES // HEAD_DIM

LOG2E = 1.4426950408889634
EXP_UNDERFLOW = 105.0
MASKED_LOGIT = -1e30

VMEM_LIMIT = 60 * 1024 * 1024


def _resident(shape):
    nd = len(shape)
    return pl.BlockSpec(shape, lambda *_: (0,) * nd, pipeline_mode=pl.Buffered(1))


def _rmsnorm(x, g):
    return x * lax.rsqrt(jnp.mean(x * x, axis=-1, keepdims=True) + NORM_EPS) * g


def _sigmoid(x):
    return 0.5 * jnp.tanh(0.5 * x) + 0.5


def _bdot(a, b):
    return jnp.dot(a, b, preferred_element_type=F32)


GELU_K = (2.0 / 3.141592653589793) ** 0.5


def _twice_gelu_tanh(x):
    inner = x * ((GELU_K * 0.044715) * (x * x) + GELU_K)
    return x * (jnp.tanh(inner) + 1.0)


def _shift_rows(padded, back):
    return pltpu.roll(padded, back, axis=0)[SUBLANES:]


def _in_lru_kernel(x_ref, g_ref, w_hbm, cw_ref, cb_ref, wr_ref, br_ref, wi_ref, bi_ref, lam_ref,
                   *rest):
    n_cast = (len(rest) - 14) // 2
    cast_srcs, rest = rest[:n_cast], rest[n_cast:]
    q_ref, k_ref, v_ref, gl_ref, ga_ref, y_ref = rest[:6]
    cast_dsts, scratch = rest[6:6 + n_cast], rest[6 + n_cast:]
    tail, a_scr, u_scr, h_scr, hprev, w_ref, w_buf, w_sem = scratch
    ts, d = x_ref.shape

    @pl.when((pl.program_id(0) == 0) & (pl.program_id(1) == 0))
    def _():
        rows = w_buf.shape[1]
        chunks = w_hbm.shape[0] // rows

        def chunk_copy(c, slot):
            return pltpu.make_async_copy(
                w_hbm.at[pl.ds(c * rows, rows)], w_buf.at[slot], w_sem.at[slot])

        chunk_copy(0, 0).start()

        def convert(c, carry):
            slot = c % 2

            @pl.when(c + 1 < chunks)
            def _():
                chunk_copy(c + 1, 1 - slot).start()

            chunk_copy(c, slot).wait()
            w_ref[pl.ds(pl.multiple_of(c * rows, rows), rows), :] = w_buf[slot].astype(BF16)
            return carry

        lax.fori_loop(0, chunks, convert, 0)

    for src, dst in zip(cast_srcs, cast_dsts):
        dst[...] = src[...].astype(dst.dtype)

    @pl.when(pl.program_id(1) == 0)
    def _():
        tail[...] = jnp.zeros(tail.shape, F32)
        hprev[...] = jnp.zeros(hprev.shape, F32)

    h = _rmsnorm(x_ref[...], g_ref[...]).astype(BF16)

    def proj(n):
        return _bdot(h, w_ref[:, n * d:(n + 1) * d])

    outs = {2: q_ref, 3: k_ref, 4: v_ref, 5: gl_ref, 6: ga_ref}

    def emit(n):
        outs[n][...] = proj(n).astype(outs[n].dtype)

    ux = proj(0)
    ug = proj(1)
    row = lax.broadcasted_iota(jnp.int32, (SUBLANES, d), 0)
    padded = jnp.concatenate([tail[...], ux], axis=0)
    c = cb_ref[...]
    for k in range(CONV_W):
        back = CONV_W - 1 - k
        xs = ux if back == 0 else _shift_rows(padded, back)
        c = c + cw_ref[k:k + 1, :] * xs
    tail[...] = ux[ts - SUBLANES:, :]
    emit(2)

    cb = c.astype(BF16)
    gr, gi = [], []
    for j in range(d // MXU_DIM):
        sl = slice(j * MXU_DIM, (j + 1) * MXU_DIM)
        gr.append(_bdot(cb[:, sl], wr_ref[j]))
        gi.append(_bdot(cb[:, sl], wi_ref[j]))
    twice_ig = jnp.tanh(jnp.concatenate(gi, axis=1) + 0.5 * bi_ref[...]) + 1.0
    emit(3)

    neg_lam = -lam_ref[...]
    softplus = jnp.maximum(neg_lam, 0.0) + jnp.log1p(jnp.exp(-jnp.abs(neg_lam)))
    half_scale = (0.5 * LRU_C) * softplus
    neg_log_a = (half_scale * jnp.tanh(jnp.concatenate(gr, axis=1) + 0.5 * br_ref[...])
                 + half_scale)
    a = jnp.exp2(neg_log_a * (-LOG2E))
    one_m_a2 = jnp.tanh(neg_log_a) * (a * a + 1.0)
    mult = jnp.where(one_m_a2 > 0.0, one_m_a2 * lax.rsqrt(one_m_a2), 0.0)
    a_scr[...] = a
    u_scr[...] = mult * (twice_ig * c)
    emit(4)

    h_in = hprev[...]
    half = SUBLANES // 2
    keeps = [(dist, row % half >= dist) for dist in (1, 2)]
    lower = row < half
    for g0 in range(0, ts, SUBLANES):
        if g0 == ts // 2:
            emit(5)
        a8 = a_scr[g0:g0 + SUBLANES, :]
        u8 = u_scr[g0:g0 + SUBLANES, :]
        for dist, keep in keeps:
            u_sh = pltpu.roll(u8, dist, axis=0)
            a_sh = pltpu.roll(a8, dist, axis=0)
            u8 = jnp.where(keep, a8 * u_sh + u8, u8)
            a8 = jnp.where(keep, a8 * a_sh, a8)
        h_lower = a8 * h_in + u8
        h_upper = a8 * h_lower[half - 1:half, :] + u8
        h8 = jnp.where(lower, h_lower, h_upper)
        h_scr[g0:g0 + SUBLANES, :] = h8
        h_in = h8[SUBLANES - 1:SUBLANES, :]
    hprev[...] = h_in
    emit(6)

    y_ref[...] = (h_scr[...] * _twice_gelu_tanh(ug)).astype(y_ref.dtype)


def _block_diag(wb, group):
    n, c, _ = wb.shape
    wg = wb.reshape(n // group, group, c, c)
    eye = jnp.eye(group, dtype=wb.dtype)
    dense = wg[:, :, :, None, :] * eye[None, :, None, :, None]
    return dense.reshape(n // group, group * c, group * c)


W_IN_CHUNK_ROWS = 64


def _in_lru(x2, g, w_in, conv_w, conv_b, w_r, b_r, w_i, b_i, lam, later_weights, batch, ts):
    t, d = x2.shape
    assert w_in.shape[0] % W_IN_CHUNK_ROWS == 0
    nt = t // batch // ts
    steps = batch * nt
    assert all(w.shape[0] % steps == 0 for w in later_weights), "slabs must tile the weights"
    slabs = [pl.BlockSpec((w.shape[0] // steps, w.shape[1]), lambda b, i: (b * nt + i, 0))
             for w in later_weights]
    group = MXU_DIM // w_r.shape[1]
    conv_w, conv_b = 0.25 * conv_w, 0.25 * conv_b
    wr = (2.0 * _block_diag(w_r, group)).astype(BF16)
    wi = (2.0 * _block_diag(w_i, group)).astype(BF16)
    tile = pl.BlockSpec((ts, d), lambda b, i: (b * nt + i, 0))
    row = _resident((1, d))
    dtypes = [BF16, BF16, BF16, F32, F32, BF16]
    outs = pl.pallas_call(
        _in_lru_kernel,
        grid=(batch, nt),
        in_specs=[tile, row, pl.BlockSpec(memory_space=pl.ANY), _resident(conv_w.shape), row,
                  _resident(wr.shape), row, _resident(wi.shape), row, row] + slabs,
        out_specs=[tile] * len(dtypes) + slabs,
        out_shape=([jax.ShapeDtypeStruct((t, d), dt) for dt in dtypes]
                   + [jax.ShapeDtypeStruct(w.shape, BF16) for w in later_weights]),
        scratch_shapes=[pltpu.VMEM((SUBLANES, d), F32), pltpu.VMEM((ts, d), F32),
                        pltpu.VMEM((ts, d), F32), pltpu.VMEM((ts, d), F32),
                        pltpu.VMEM((1, d), F32), pltpu.VMEM(w_in.shape, BF16),
                        pltpu.VMEM((2, W_IN_CHUNK_ROWS, w_in.shape[1]), F32),
                        pltpu.SemaphoreType.DMA((2,))],
        compiler_params=pltpu.CompilerParams(
            dimension_semantics=("arbitrary", "arbitrary"), vmem_limit_bytes=VMEM_LIMIT),
        name="in_lru",
    )(x2, g, w_in, conv_w, conv_b.reshape(1, d), wr, b_r.reshape(1, d), wi, b_i.reshape(1, d),
      lam.reshape(1, d), *later_weights)
    return outs[:len(dtypes)], outs[len(dtypes):]


SOFTPLUS_CAP = 80.0


def _softplus(z):
    return jnp.maximum(z, jnp.log(1.0 + jnp.exp2(jnp.minimum(z, SOFTPLUS_CAP) * LOG2E)))


LANE_BLOCKS = 4


def _attn_kernel(qp_ref, qn_ref, k_ref, v_ref, tri_ref, o_ref, z_scr, att_scr, carry_scr):
    tq = qp_ref.shape[0]
    tk = tri_ref.shape[1]
    i = pl.program_id(2)
    nq = pl.num_programs(2) - 1
    lane = lax.broadcasted_iota(jnp.int32, (tq, ATTN_LANES), 1)
    tri = tri_ref[...]
    done = jnp.maximum(i - 1, 0)

    def stack_heads(q):
        q = q * (HEAD_DIM ** -0.5)
        zero = jnp.zeros_like(q)
        return jnp.concatenate(
            [jnp.where(lane // HEAD_DIM == h, q, zero) for h in range(HEADS_PER_BLOCK)], axis=0)

    def tile_rows(j):
        return pl.ds(pl.multiple_of(j * tk, tk), tk)

    def scores(q2, k_tiles):
        return lax.dot_general(q2, k_tiles, (((1,), (1,)), ((), ())), preferred_element_type=F32)

    def decay_terms(z):
        pos = _softplus(z)
        pos_bf = pos.astype(BF16)
        later = _bdot(pos_bf, tri)
        return z - pos, later, later[:, 0:1] + pos_bf[:, 0:1].astype(F32)

    def first_scores(q2, j, lanes):
        prev = jnp.maximum(j - 1, 0)
        return scores(q2, jnp.concatenate(
            [k_ref[tile_rows(j), lanes], k_ref[tile_rows(prev), lanes]], axis=0))

    @pl.when(i == 0)
    def _():
        for lb in range(LANE_BLOCKS):
            lanes = slice(lb * ATTN_LANES, (lb + 1) * ATTN_LANES)
            z_scr[lb] = first_scores(stack_heads(qp_ref[:, lanes]), 0, lanes)
        att_scr[...] = jnp.zeros(att_scr.shape, BF16)
        carry_scr[...] = jnp.full(carry_scr.shape, 2.0 * EXP_UNDERFLOW, F32)

    def first_steps(lb):
        lanes = slice(lb * ATTN_LANES, (lb + 1) * ATTN_LANES)

        carry_done = carry_scr[lb]
        v_prev = v_ref[tile_rows(jnp.maximum(done - 1, 0)), lanes]
        v_both = jnp.concatenate(
            [v_ref[tile_rows(done), lanes], jnp.where(done > 0, v_prev, jnp.zeros_like(v_prev))],
            axis=0)
        acc = _bdot(att_scr[lb], v_both)

        z = z_scr[lb]
        z_scr[lb] = first_scores(stack_heads(qn_ref[:, lanes]), jnp.minimum(i + 1, nq - 1),
                                 lanes)
        rq = lax.broadcasted_iota(jnp.int32, (HEADS_PER_BLOCK * tq, tk), 0) % tq
        ck = lax.broadcasted_iota(jnp.int32, (HEADS_PER_BLOCK * tq, tk), 1)
        z_diag = jnp.where(ck < rq, z[:, :tk], MASKED_LOGIT)
        z_prev = z[:, tk:]
        rows = HEADS_PER_BLOCK * tq
        log_beta, later, total = decay_terms(jnp.concatenate([z_diag, z_prev], axis=0))
        carry = total[:rows]
        att_scr[lb] = jnp.concatenate(
            [jnp.exp(log_beta[:rows] - later[:rows]),
             jnp.exp(log_beta[rows:] - (later[rows:] + carry))], axis=1).astype(BF16)
        carry_scr[lb] = carry + total[rows:]
        return acc, carry_done, jnp.min(carry_done)

    def earlier_tiles(lb, acc, carry_done, low):
        lanes = slice(lb * ATTN_LANES, (lb + 1) * ATTN_LANES)

        def cond(state):
            j, low, _, _ = state
            return jnp.logical_and(j >= 0, low < EXP_UNDERFLOW)

        def body(state):
            j, _, acc, carry = state
            z = scores(stack_heads(qp_ref[:, lanes]), k_ref[tile_rows(j), lanes])
            log_beta, later, total = decay_terms(z)
            att = jnp.exp(log_beta - (later + carry))
            acc = acc + _bdot(att.astype(BF16), v_ref[tile_rows(j), lanes])
            carry = carry + total
            return j - 1, jnp.min(carry), acc, carry

        _, _, acc, _ = lax.while_loop(cond, body, (i - 3, low, acc, carry_done))
        out = acc[:tq]
        for h in range(1, HEADS_PER_BLOCK):
            out = jnp.where(lane // HEAD_DIM == h, acc[h * tq:(h + 1) * tq], out)
        o_ref[:, lanes] = out.astype(o_ref.dtype)

    started = [first_steps(lb) for lb in range(LANE_BLOCKS)]
    for lb, (acc, carry_done, low) in enumerate(started):
        earlier_tiles(lb, acc, carry_done, low)


def _attn(q, k, v, batch, tq):
    t, width = q.shape
    seq = t // batch
    nq = seq // tq
    tk = tq
    group = LANE_BLOCKS * ATTN_LANES
    later = jnp.arange(tk)[:, None] > jnp.arange(tk)[None, :]

    def q_tile(shift):
        return pl.BlockSpec(
            (tq, group), lambda b, hp, i: (b * nq + jnp.clip(i + shift, 0, nq - 1), hp))

    kv_spec = pl.BlockSpec((seq, group), lambda b, hp, i: (b, hp))
    rows = HEADS_PER_BLOCK * tq
    return pl.pallas_call(
        _attn_kernel,
        grid=(batch, width // group, nq + 1),
        in_specs=[q_tile(-1), q_tile(1), kv_spec, kv_spec, _resident((tk, tk))],
        out_specs=q_tile(-1),
        out_shape=jax.ShapeDtypeStruct((t, width), BF16),
        scratch_shapes=[pltpu.VMEM((LANE_BLOCKS, rows, 2 * tk), F32),
                        pltpu.VMEM((LANE_BLOCKS, rows, 2 * tk), BF16),
                        pltpu.VMEM((LANE_BLOCKS, rows, 1), F32)],
        compiler_params=pltpu.CompilerParams(
            dimension_semantics=("arbitrary", "arbitrary", "arbitrary"),
            vmem_limit_bytes=VMEM_LIMIT),
        name="attn",
    )(q, q, k, v, later.astype(BF16))


def _post_kernel(x_ref, yl_ref, ya_ref, gl_ref, ga_ref, p_ref, wbl_ref, wba_ref, wo_ref,
                 g_mlp_ref, wu_ref, wd_ref, g_ple_ref, wpg_ref, wpl_ref, g_fin_ref, o_ref):
    merged = (_sigmoid(gl_ref[...]) * _bdot(yl_ref[...], wbl_ref[...])
              + _sigmoid(ga_ref[...]) * _bdot(ya_ref[...], wba_ref[...]))
    x = x_ref[...] + _bdot(merged.astype(BF16), wo_ref[...])
    h2 = _rmsnorm(x, g_mlp_ref[...]).astype(BF16)
    up = jnp.square(jnp.maximum(_bdot(h2, wu_ref[...]), 0.0))
    x = x + _bdot(up.astype(BF16), wd_ref[...])
    h3 = _rmsnorm(x, g_ple_ref[...]).astype(BF16)
    gate = _sigmoid(_bdot(h3, wpg_ref[...]))
    x = x + gate * _bdot(p_ref[...].astype(BF16), wpl_ref[...])
    o_ref[...] = _rmsnorm(x, g_fin_ref[...])


def _post(x2, y_lru, y_att, g_lru, g_att, p2, w_bl, w_ba, w_o, g_mlp, w_up, w_down, g_ple, w_pg,
          w_pl, g_fin, tm):
    t, d = x2.shape
    tile = pl.BlockSpec((tm, d), lambda i: (i, 0))
    ptile = pl.BlockSpec((tm, p2.shape[1]), lambda i: (i, 0))
    row = _resident((1, d))
    square = _resident((d, d))
    return pl.pallas_call(
        _post_kernel,
        grid=(t // tm,),
        in_specs=[tile] * 5 + [ptile, square, square, square, row, _resident(w_up.shape),
                               _resident(w_down.shape), row, square, _resident(w_pl.shape), row],
        out_specs=tile,
        out_shape=jax.ShapeDtypeStruct((t, d), F32),
        compiler_params=pltpu.CompilerParams(
            dimension_semantics=("arbitrary",), vmem_limit_bytes=VMEM_LIMIT),
        name="post",
    )(x2, y_lru, y_att, g_lru, g_att, p2, w_bl, w_ba, w_o, g_mlp, w_up, w_down, g_ple, w_pg,
      w_pl, g_fin)


def kernel(x, p, norm_mix_g, w_in, conv_w, conv_b, w_rgate, b_rgate, w_igate, b_igate, lru_lambda, w_br_lru, w_br_att, w_out, norm_mlp_g, w_mlp_up, w_mlp_down, norm_ple_g, w_ple_gate, w_ple, norm_final_g):
    batch, seq, d = x.shape
    depth = w_in.shape[0]
    t = batch * seq
    x2 = x.reshape(t, d)
    for l in range(depth):
        later = [w_br_lru[l], w_br_att[l], w_out[l], w_mlp_up[l], w_mlp_down[l], w_ple_gate[l]]
        (q, k, v, g_lru, g_att, y_lru), later_bf = _in_lru(
            x2, norm_mix_g[l].reshape(1, d), w_in[l], conv_w[l], conv_b[l],
            w_rgate[l], b_rgate[l], w_igate[l], b_igate[l], lru_lambda[l], later, batch, ts=512)
        w_bl, w_ba, w_o, w_up, w_down, w_pg = later_bf
        y_att = _attn(q, k, v, batch, tq=256)
        assert l == depth - 1, "deeper stacks need a post variant without the final norm"
        x2 = _post(x2, y_lru, y_att, g_lru, g_att, p[l].reshape(t, -1), w_bl, w_ba, w_o,
                   norm_mlp_g[l].reshape(1, d), w_up, w_down, norm_ple_g[l].reshape(1, d),
                   w_pg, w_ple[l].astype(BF16), norm_final_g.reshape(1, d), tm=512)
    return x2.reshape(batch, seq, d)
```

```python
import jax
import jax.numpy as jnp
from jax import lax
from jax.experimental import pallas as pl
from jax.experimental.pallas import tpu as pltpu

F32 = jnp.float32
BF16 = jnp.bfloat16

NORM_EPS = 1e-6
LRU_C = 8.0
CONV_W = 4
HEAD_DIM = 64
SUBLANES = 8
MXU_DIM = 256
ATTN_LANES = 128
HEADS_PER_BLOCK = ATTN_LANES // HEAD_DIM

LOG2E = 1.4426950408889634
EXP_UNDERFLOW = 105.0
MASKED_LOGIT = -1e30

VMEM_LIMIT = 60 * 1024 * 1024


def _resident(shape):
    nd = len(shape)
    return pl.BlockSpec(shape, lambda *_: (0,) * nd, pipeline_mode=pl.Buffered(1))


def _rmsnorm(x, g):
    return x * lax.rsqrt(jnp.mean(x * x, axis=-1, keepdims=True) + NORM_EPS) * g


def _sigmoid(x):
    return 0.5 * jnp.tanh(0.5 * x) + 0.5


def _bdot(a, b):
    return jnp.dot(a, b, preferred_element_type=F32)


GELU_K = (2.0 / 3.141592653589793) ** 0.5


def _twice_gelu_tanh(x):
    inner = x * ((GELU_K * 0.044715) * (x * x) + GELU_K)
    return x * (jnp.tanh(inner) + 1.0)


def _shift_rows(padded, back):
    return pltpu.roll(padded, back, axis=0)[SUBLANES:]


SCAN_PROJECTIONS = (5, 6)


def _in_lru_kernel(x_ref, g_ref, w_hbm, cw_ref, cb_ref, wr_ref, br_ref, wi_ref, bi_ref, lam_ref,
                   *rest):
    n_cast = (len(rest) - 15) // 2
    cast_srcs, rest = rest[:n_cast], rest[n_cast:]
    q_ref, k_ref, v_ref, gl_ref, ga_ref, y_ref = rest[:6]
    cast_dsts, scratch = rest[6:6 + n_cast], rest[6 + n_cast:]
    tail, a_scr, a_odd, u_scr, h_scr, hprev, w_ref, w_buf, w_sem = scratch
    ts, d = x_ref.shape

    @pl.when((pl.program_id(0) == 0) & (pl.program_id(1) == 0))
    def _():
        rows = w_buf.shape[1]
        chunks = w_hbm.shape[0] // rows

        def chunk_copy(c, slot):
            return pltpu.make_async_copy(
                w_hbm.at[pl.ds(c * rows, rows)], w_buf.at[slot], w_sem.at[slot])

        chunk_copy(0, 0).start()

        def convert(c, carry):
            slot = c % 2

            @pl.when(c + 1 < chunks)
            def _():
                chunk_copy(c + 1, 1 - slot).start()

            chunk_copy(c, slot).wait()
            w_ref[pl.ds(pl.multiple_of(c * rows, rows), rows), :] = w_buf[slot].astype(BF16)
            return carry

        lax.fori_loop(0, chunks, convert, 0)

    for src, dst in zip(cast_srcs, cast_dsts):
        dst[...] = src[...].astype(dst.dtype)

    @pl.when(pl.program_id(1) == 0)
    def _():
        tail[...] = jnp.zeros(tail.shape, F32)
        hprev[...] = jnp.zeros(hprev.shape, F32)

    h = _rmsnorm(x_ref[...], g_ref[...]).astype(BF16)

    def proj(n):
        return _bdot(h, w_ref[:, n * d:(n + 1) * d])

    outs = {2: q_ref, 3: k_ref, 4: v_ref, 5: gl_ref, 6: ga_ref}

    def emit(n):
        outs[n][...] = proj(n).astype(outs[n].dtype)

    ux = proj(0)
    ug = proj(1)
    row = lax.broadcasted_iota(jnp.int32, (SUBLANES, d), 0)
    padded = jnp.concatenate([tail[...], ux], axis=0)
    c = cb_ref[...]
    for k in range(CONV_W):
        back = CONV_W - 1 - k
        xs = ux if back == 0 else _shift_rows(padded, back)
        c = c + cw_ref[k:k + 1, :] * xs
    tail[...] = ux[ts - SUBLANES:, :]
    emit(2)

    cb = c.astype(BF16)
    gr, gi = [], []
    for j in range(d // MXU_DIM):
        sl = slice(j * MXU_DIM, (j + 1) * MXU_DIM)
        gr.append(_bdot(cb[:, sl], wr_ref[j]))
        gi.append(_bdot(cb[:, sl], wi_ref[j]))
    twice_ig = jnp.tanh(jnp.concatenate(gi, axis=1) + 0.5 * bi_ref[...]) + 1.0
    emit(3)

    neg_lam = -lam_ref[...]
    softplus = jnp.maximum(neg_lam, 0.0) + jnp.log1p(jnp.exp(-jnp.abs(neg_lam)))
    half_scale = (0.5 * LRU_C) * softplus
    neg_log_a = (half_scale * jnp.tanh(jnp.concatenate(gr, axis=1) + 0.5 * br_ref[...])
                 + half_scale)
    a = jnp.exp2(neg_log_a * (-LOG2E))
    one_m_a2 = jnp.tanh(neg_log_a) * (a * a + 1.0)
    mult = jnp.where(one_m_a2 > 0.0, one_m_a2 * lax.rsqrt(one_m_a2), 0.0)
    pieces = [(n, j) for n in SCAN_PROJECTIONS for j in range(d // MXU_DIM)]
    stretch = ts // len(pieces)
    a_bufs = (a_scr, a_odd)
    for s in range(len(pieces)):
        a_bufs[s % 2][(s // 2) * stretch:(s // 2 + 1) * stretch, :] = (
            a[s * stretch:(s + 1) * stretch])
    u_scr[...] = mult * (twice_ig * c)
    emit(4)
    unseen_zero = jnp.minimum(pl.program_id(1), 0)
    spare = ts // 2

    def emit_piece(s):
        n, j = pieces[s]
        cols = slice(j * MXU_DIM, (j + 1) * MXU_DIM)
        part = _bdot(h, w_ref[:, n * d + j * MXU_DIM:n * d + (j + 1) * MXU_DIM])
        outs[n][:, cols] = part.astype(outs[n].dtype)
        a_bufs[s % 2][spare:spare + SUBLANES, cols] = part[:SUBLANES]

    h_in = hprev[...]
    half = SUBLANES // 2
    keeps = [(dist, row % half >= dist) for dist in (1, 2)]
    lower = row < half
    for g0 in range(0, ts, SUBLANES):
        s = g0 // stretch
        if g0 % stretch == 0:
            emit_piece(s)
        local = (s // 2) * stretch + g0 % stretch
        a8 = a_bufs[s % 2][pl.ds(pl.multiple_of(local + unseen_zero, SUBLANES), SUBLANES), :]
        u8 = u_scr[g0:g0 + SUBLANES, :]
        for dist, keep in keeps:
            u_sh = pltpu.roll(u8, dist, axis=0)
            a_sh = pltpu.roll(a8, dist, axis=0)
            u8 = jnp.where(keep, a8 * u_sh + u8, u8)
            a8 = jnp.where(keep, a8 * a_sh, a8)
        h_lower = a8 * h_in + u8
        h_upper = a8 * h_lower[half - 1:half, :] + u8
        h8 = jnp.where(lower, h_lower, h_upper)
        h_scr[g0:g0 + SUBLANES, :] = h8
        h_in = h8[SUBLANES - 1:SUBLANES, :]
    hprev[...] = h_in

    y_ref[...] = (h_scr[...] * _twice_gelu_tanh(ug)).astype(y_ref.dtype)


def _block_diag(wb, group):
    n, c, _ = wb.shape
    wg = wb.reshape(n // group, group, c, c)
    eye = jnp.eye(group, dtype=wb.dtype)
    dense = wg[:, :, :, None, :] * eye[None, :, None, :, None]
    return dense.reshape(n // group, group * c, group * c)


W_IN_CHUNK_ROWS = 64


def _in_lru(x2, g, w_in, conv_w, conv_b, w_r, b_r, w_i, b_i, lam, later_weights, batch, ts):
    t, d = x2.shape
    assert w_in.shape[0] % W_IN_CHUNK_ROWS == 0
    nt = t // batch // ts
    steps = batch * nt
    assert all(w.shape[0] % steps == 0 for w in later_weights), "slabs must tile the weights"
    slabs = [pl.BlockSpec((w.shape[0] // steps, w.shape[1]), lambda b, i: (b * nt + i, 0))
             for w in later_weights]
    group = MXU_DIM // w_r.shape[1]
    conv_w, conv_b = 0.25 * conv_w, 0.25 * conv_b
    wr = (2.0 * _block_diag(w_r, group)).astype(BF16)
    wi = (2.0 * _block_diag(w_i, group)).astype(BF16)
    tile = pl.BlockSpec((ts, d), lambda b, i: (b * nt + i, 0))
    row = _resident((1, d))
    dtypes = [BF16, BF16, BF16, F32, F32, BF16]
    outs = pl.pallas_call(
        _in_lru_kernel,
        grid=(batch, nt),
        in_specs=[tile, row, pl.BlockSpec(memory_space=pl.ANY), _resident(conv_w.shape), row,
                  _resident(wr.shape), row, _resident(wi.shape), row, row] + slabs,
        out_specs=[tile] * len(dtypes) + slabs,
        out_shape=([jax.ShapeDtypeStruct((t, d), dt) for dt in dtypes]
                   + [jax.ShapeDtypeStruct(w.shape, BF16) for w in later_weights]),
        scratch_shapes=[pltpu.VMEM((SUBLANES, d), F32),
                        pltpu.VMEM((ts // 2 + SUBLANES, d), F32),
                        pltpu.VMEM((ts // 2 + SUBLANES, d), F32),
                        pltpu.VMEM((ts, d), F32), pltpu.VMEM((ts, d), F32),
                        pltpu.VMEM((1, d), F32), pltpu.VMEM(w_in.shape, BF16),
                        pltpu.VMEM((2, W_IN_CHUNK_ROWS, w_in.shape[1]), F32),
                        pltpu.SemaphoreType.DMA((2,))],
        compiler_params=pltpu.CompilerParams(
            dimension_semantics=("arbitrary", "arbitrary"), vmem_limit_bytes=VMEM_LIMIT),
        name="in_lru",
    )(x2, g, w_in, conv_w, conv_b.reshape(1, d), wr, b_r.reshape(1, d), wi, b_i.reshape(1, d),
      lam.reshape(1, d), *later_weights)
    return outs[:len(dtypes)], outs[len(dtypes):]


SOFTPLUS_CAP = 80.0


def _softplus(z):
    return jnp.maximum(z, jnp.log(1.0 + jnp.exp2(jnp.minimum(z, SOFTPLUS_CAP) * LOG2E)))


LANE_BLOCKS = 4


def _attn_kernel(qp_ref, qn_ref, k_ref, v_ref, tri_ref, o_ref, z_scr, att_scr, carry_scr):
    tq = qp_ref.shape[0]
    tk = tri_ref.shape[1]
    i = pl.program_id(2)
    nq = pl.num_programs(2) - 1
    lane = lax.broadcasted_iota(jnp.int32, (tq, ATTN_LANES), 1)
    tri = tri_ref[...]
    done = jnp.maximum(i - 1, 0)

    def stack_heads(q):
        q = q * (HEAD_DIM ** -0.5)
        zero = jnp.zeros_like(q)
        return jnp.concatenate(
            [jnp.where(lane // HEAD_DIM == h, q, zero) for h in range(HEADS_PER_BLOCK)], axis=0)

    def tile_rows(j):
        return pl.ds(pl.multiple_of(j * tk, tk), tk)

    def scores(q2, k_tiles):
        return lax.dot_general(q2, k_tiles, (((1,), (1,)), ((), ())), preferred_element_type=F32)

    def decay_terms(z):
        pos = _softplus(z)
        pos_bf = pos.astype(BF16)
        later = _bdot(pos_bf, tri)
        return z - pos, later, later[:, 0:1] + pos_bf[:, 0:1].astype(F32)

    def first_scores(q2, j, lanes):
        prev = jnp.maximum(j - 1, 0)
        return scores(q2, jnp.concatenate(
            [k_ref[tile_rows(j), lanes], k_ref[tile_rows(prev), lanes]], axis=0))

    @pl.when(i == 0)
    def _():
        for lb in range(LANE_BLOCKS):
            lanes = slice(lb * ATTN_LANES, (lb + 1) * ATTN_LANES)
            z_scr[lb] = first_scores(stack_heads(qp_ref[:, lanes]), 0, lanes)
        att_scr[...] = jnp.zeros(att_scr.shape, BF16)
        carry_scr[...] = jnp.full(carry_scr.shape, 2.0 * EXP_UNDERFLOW, F32)

    def first_steps(lb):
        lanes = slice(lb * ATTN_LANES, (lb + 1) * ATTN_LANES)

        carry_done = carry_scr[lb]
        v_prev = v_ref[tile_rows(jnp.maximum(done - 1, 0)), lanes]
        v_both = jnp.concatenate(
            [v_ref[tile_rows(done), lanes], jnp.where(done > 0, v_prev, jnp.zeros_like(v_prev))],
            axis=0)
        acc = _bdot(att_scr[lb], v_both)

        z = z_scr[lb]
        z_scr[lb] = first_scores(stack_heads(qn_ref[:, lanes]), jnp.minimum(i + 1, nq - 1),
                                 lanes)
        rq = lax.broadcasted_iota(jnp.int32, (HEADS_PER_BLOCK * tq, tk), 0) % tq
        ck = lax.broadcasted_iota(jnp.int32, (HEADS_PER_BLOCK * tq, tk), 1)
        z_diag = jnp.where(ck < rq, z[:, :tk], MASKED_LOGIT)
        z_prev = z[:, tk:]
        rows = HEADS_PER_BLOCK * tq
        log_beta, later, total = decay_terms(jnp.concatenate([z_diag, z_prev], axis=0))
        carry = total[:rows]
        att_scr[lb] = jnp.concatenate(
            [jnp.exp(log_beta[:rows] - later[:rows]),
             jnp.exp(log_beta[rows:] - (later[rows:] + carry))], axis=1).astype(BF16)
        carry_scr[lb] = carry + total[rows:]
        return acc, carry_done, jnp.min(carry_done)

    def earlier_tiles(lb, acc, carry_done, low):
        lanes = slice(lb * ATTN_LANES, (lb + 1) * ATTN_LANES)

        def cond(state):
            j, low, _, _ = state
            return jnp.logical_and(j >= 0, low < EXP_UNDERFLOW)

        def body(state):
            j, _, acc, carry = state
            z = scores(stack_heads(qp_ref[:, lanes]), k_ref[tile_rows(j), lanes])
            log_beta, later, total = decay_terms(z)
            att = jnp.exp(log_beta - (later + carry))
            acc = acc + _bdot(att.astype(BF16), v_ref[tile_rows(j), lanes])
            carry = carry + total
            return j - 1, jnp.min(carry), acc, carry

        _, _, acc, _ = lax.while_loop(cond, body, (i - 3, low, acc, carry_done))
        out = acc[:tq]
        for h in range(1, HEADS_PER_BLOCK):
            out = jnp.where(lane // HEAD_DIM == h, acc[h * tq:(h + 1) * tq], out)
        o_ref[:, lanes] = out.astype(o_ref.dtype)

    started = [first_steps(lb) for lb in range(LANE_BLOCKS)]
    for lb, (acc, carry_done, low) in enumerate(started):
        earlier_tiles(lb, acc, carry_done, low)


def _attn(q, k, v, batch, tq):
    t, width = q.shape
    seq = t // batch
    nq = seq // tq
    tk = tq
    group = LANE_BLOCKS * ATTN_LANES
    later = jnp.arange(tk)[:, None] > jnp.arange(tk)[None, :]

    def q_tile(shift):
        return pl.BlockSpec(
            (tq, group), lambda b, hp, i: (b * nq + jnp.clip(i + shift, 0, nq - 1), hp))

    kv_spec = pl.BlockSpec((seq, group), lambda b, hp, i: (b, hp))
    rows = HEADS_PER_BLOCK * tq
    return pl.pallas_call(
        _attn_kernel,
        grid=(batch, width // group, nq + 1),
        in_specs=[q_tile(-1), q_tile(1), kv_spec, kv_spec, _resident((tk, tk))],
        out_specs=q_tile(-1),
        out_shape=jax.ShapeDtypeStruct((t, width), BF16),
        scratch_shapes=[pltpu.VMEM((LANE_BLOCKS, rows, 2 * tk), F32),
                        pltpu.VMEM((LANE_BLOCKS, rows, 2 * tk), BF16),
                        pltpu.VMEM((LANE_BLOCKS, rows, 1), F32)],
        compiler_params=pltpu.CompilerParams(
            dimension_semantics=("arbitrary", "arbitrary", "arbitrary"),
            vmem_limit_bytes=VMEM_LIMIT),
        name="attn",
    )(q, q, k, v, later.astype(BF16))


def _post_kernel(x_ref, yl_ref, ya_ref, gl_ref, ga_ref, p_ref, wbl_ref, wba_ref, wo_ref,
                 g_mlp_ref, wu_ref, wd_ref, g_ple_ref, wpg_ref, wpl_ref, g_fin_ref, o_ref):
    merged = (_sigmoid(gl_ref[...]) * _bdot(yl_ref[...], wbl_ref[...])
              + _sigmoid(ga_ref[...]) * _bdot(ya_ref[...], wba_ref[...]))
    x = x_ref[...] + _bdot(merged.astype(BF16), wo_ref[...])
    h2 = _rmsnorm(x, g_mlp_ref[...]).astype(BF16)
    up = jnp.square(jnp.maximum(_bdot(h2, wu_ref[...]), 0.0))
    x = x + _bdot(up.astype(BF16), wd_ref[...])
    h3 = _rmsnorm(x, g_ple_ref[...]).astype(BF16)
    gate = _sigmoid(_bdot(h3, wpg_ref[...]))
    x = x + gate * _bdot(p_ref[...].astype(BF16), wpl_ref[...])
    o_ref[...] = _rmsnorm(x, g_fin_ref[...])


def _post(x2, y_lru, y_att, g_lru, g_att, p2, w_bl, w_ba, w_o, g_mlp, w_up, w_down, g_ple, w_pg,
          w_pl, g_fin, tm):
    t, d = x2.shape
    tile = pl.BlockSpec((tm, d), lambda i: (i, 0))
    ptile = pl.BlockSpec((tm, p2.shape[1]), lambda i: (i, 0))
    row = _resident((1, d))
    square = _resident((d, d))
    return pl.pallas_call(
        _post_kernel,
        grid=(t // tm,),
        in_specs=[tile] * 5 + [ptile, square, square, square, row, _resident(w_up.shape),
                               _resident(w_down.shape), row, square, _resident(w_pl.shape), row],
        out_specs=tile,
        out_shape=jax.ShapeDtypeStruct((t, d), F32),
        compiler_params=pltpu.CompilerParams(
            dimension_semantics=("arbitrary",), vmem_limit_bytes=VMEM_LIMIT),
        name="post",
    )(x2, y_lru, y_att, g_lru, g_att, p2, w_bl, w_ba, w_o, g_mlp, w_up, w_down, g_ple, w_pg,
      w_pl, g_fin)


def kernel(x, p, norm_mix_g, w_in, conv_w, conv_b, w_rgate, b_rgate, w_igate, b_igate, lru_lambda, w_br_lru, w_br_att, w_out, norm_mlp_g, w_mlp_up, w_mlp_down, norm_ple_g, w_ple_gate, w_ple, norm_final_g):
    batch, seq, d = x.shape
    depth = w_in.shape[0]
    t = batch * seq
    x2 = x.reshape(t, d)
    for l in range(depth):
        later = [w_br_lru[l], w_br_att[l], w_out[l], w_mlp_up[l], w_mlp_down[l], w_ple_gate[l]]
        (q, k, v, g_lru, g_att, y_lru), later_bf = _in_lru(
            x2, norm_mix_g[l].reshape(1, d), w_in[l], conv_w[l], conv_b[l],
            w_rgate[l], b_rgate[l], w_igate[l], b_igate[l], lru_lambda[l], later, batch, ts=512)
        w_bl, w_ba, w_o, w_up, w_down, w_pg = later_bf
        y_att = _attn(q, k, v, batch, tq=256)
        assert l == depth - 1, "deeper stacks need a post variant without the final norm"
        x2 = _post(x2, y_lru, y_att, g_lru, g_att, p[l].reshape(t, -1), w_bl, w_ba, w_o,
                   norm_mlp_g[l].reshape(1, d), w_up, w_down, norm_ple_g[l].reshape(1, d),
                   w_pg, w_ple[l].astype(BF16), norm_final_g.reshape(1, d), tm=512)
    return x2.reshape(batch, seq, d)
```

```python
import jax
import jax.numpy as jnp
from jax import lax
from jax.experimental import pallas as pl
from jax.experimental.pallas import tpu as pltpu

F32 = jnp.float32
BF16 = jnp.bfloat16

NORM_EPS = 1e-6
LRU_C = 8.0
CONV_W = 4
HEAD_DIM = 64
SUBLANES = 8
MXU_DIM = 256
ATTN_LANES = 128
HEADS_PER_BLOCK = ATTN_LANES // HEAD_DIM

LOG2E = 1.4426950408889634
EXP_UNDERFLOW = 105.0
MASKED_LOGIT = -1e30

VMEM_LIMIT = 60 * 1024 * 1024


def _resident(shape):
    nd = len(shape)
    return pl.BlockSpec(shape, lambda *_: (0,) * nd, pipeline_mode=pl.Buffered(1))


def _rmsnorm(x, g):
    return x * lax.rsqrt(jnp.mean(x * x, axis=-1, keepdims=True) + NORM_EPS) * g


def _sigmoid(x):
    return 0.5 * jnp.tanh(0.5 * x) + 0.5


def _bdot(a, b):
    return jnp.dot(a, b, preferred_element_type=F32)


GELU_K = (2.0 / 3.141592653589793) ** 0.5


def _twice_gelu_tanh(x):
    inner = x * ((GELU_K * 0.044715) * (x * x) + GELU_K)
    return x * (jnp.tanh(inner) + 1.0)


def _shift_rows(padded, back):
    return pltpu.roll(padded, back, axis=0)[SUBLANES:]


def _in_lru_kernel(x_ref, g_ref, w_hbm, cw_ref, cb_ref, wr_ref, br_ref, wi_ref, bi_ref, lam_ref,
                   *rest):
    n_cast = (len(rest) - 14) // 2
    cast_srcs, rest = rest[:n_cast], rest[n_cast:]
    q_ref, k_ref, v_ref, gl_ref, ga_ref, y_ref = rest[:6]
    cast_dsts, scratch = rest[6:6 + n_cast], rest[6 + n_cast:]
    tail, a_scr, u_scr, h_scr, hprev, w_ref, w_buf, w_sem = scratch
    ts, d = x_ref.shape

    @pl.when((pl.program_id(0) == 0) & (pl.program_id(1) == 0))
    def _():
        rows = w_buf.shape[1]
        chunks = w_hbm.shape[0] // rows

        def chunk_copy(c, slot):
            return pltpu.make_async_copy(
                w_hbm.at[pl.ds(c * rows, rows)], w_buf.at[slot], w_sem.at[slot])

        chunk_copy(0, 0).start()

        def convert(c, carry):
            slot = c % 2

            @pl.when(c + 1 < chunks)
            def _():
                chunk_copy(c + 1, 1 - slot).start()

            chunk_copy(c, slot).wait()
            w_ref[pl.ds(pl.multiple_of(c * rows, rows), rows), :] = w_buf[slot].astype(BF16)
            return carry

        lax.fori_loop(0, chunks, convert, 0)

    for src, dst in zip(cast_srcs, cast_dsts):
        dst[...] = src[...].astype(dst.dtype)

    @pl.when(pl.program_id(1) == 0)
    def _():
        tail[...] = jnp.zeros(tail.shape, F32)
        hprev[...] = jnp.zeros(hprev.shape, F32)

    h = _rmsnorm(x_ref[...], g_ref[...]).astype(BF16)

    def proj(n):
        return _bdot(h, w_ref[:, n * d:(n + 1) * d])

    outs = {2: q_ref, 3: k_ref, 4: v_ref, 5: gl_ref, 6: ga_ref}

    def emit(n):
        outs[n][...] = proj(n).astype(outs[n].dtype)

    ux = proj(0)
    ug = proj(1)
    row = lax.broadcasted_iota(jnp.int32, (SUBLANES, d), 0)
    padded = jnp.concatenate([tail[...], ux], axis=0)
    c = cb_ref[...]
    for k in range(CONV_W):
        back = CONV_W - 1 - k
        xs = ux if back == 0 else _shift_rows(padded, back)
        c = c + cw_ref[k:k + 1, :] * xs
    tail[...] = ux[ts - SUBLANES:, :]
    emit(2)

    cb = c.astype(BF16)
    gr, gi = [], []
    for j in range(d // MXU_DIM):
        sl = slice(j * MXU_DIM, (j + 1) * MXU_DIM)
        gr.append(_bdot(cb[:, sl], wr_ref[j]))
        gi.append(_bdot(cb[:, sl], wi_ref[j]))
    twice_ig = jnp.tanh(jnp.concatenate(gi, axis=1) + 0.5 * bi_ref[...]) + 1.0
    emit(3)

    neg_lam = -lam_ref[...]
    softplus = jnp.maximum(neg_lam, 0.0) + jnp.log1p(jnp.exp(-jnp.abs(neg_lam)))
    half_scale = (0.5 * LRU_C) * softplus
    neg_log_a = (half_scale * jnp.tanh(jnp.concatenate(gr, axis=1) + 0.5 * br_ref[...])
                 + half_scale)
    a = jnp.exp2(neg_log_a * (-LOG2E))
    one_m_a2 = jnp.tanh(neg_log_a) * (a * a + 1.0)
    mult = jnp.where(one_m_a2 > 0.0, one_m_a2 * lax.rsqrt(one_m_a2), 0.0)
    a_scr[...] = a
    u_scr[...] = mult * (twice_ig * c)
    emit(4)

    h_in = hprev[...]
    half = SUBLANES // 2
    keeps = [(dist, row % half >= dist) for dist in (1, 2)]
    lower = row < half
    for g0 in range(0, ts, SUBLANES):
        if g0 == ts // 2:
            emit(5)
        a8 = a_scr[g0:g0 + SUBLANES, :]
        u8 = u_scr[g0:g0 + SUBLANES, :]
        for dist, keep in keeps:
            u_sh = pltpu.roll(u8, dist, axis=0)
            a_sh = pltpu.roll(a8, dist, axis=0)
            u8 = jnp.where(keep, a8 * u_sh + u8, u8)
            a8 = jnp.where(keep, a8 * a_sh, a8)
        h_lower = a8 * h_in + u8
        h_upper = a8 * h_lower[half - 1:half, :] + u8
        h8 = jnp.where(lower, h_lower, h_upper)
        h_scr[g0:g0 + SUBLANES, :] = h8
        h_in = h8[SUBLANES - 1:SUBLANES, :]
    hprev[...] = h_in
    emit(6)

    y_ref[...] = (h_scr[...] * _twice_gelu_tanh(ug)).astype(y_ref.dtype)


def _block_diag(wb, group):
    n, c, _ = wb.shape
    wg = wb.reshape(n // group, group, c, c)
    eye = jnp.eye(group, dtype=wb.dtype)
    dense = wg[:, :, :, None, :] * eye[None, :, None, :, None]
    return dense.reshape(n // group, group * c, group * c)


W_IN_CHUNK_ROWS = 64


def _in_lru(x2, g, w_in, conv_w, conv_b, w_r, b_r, w_i, b_i, lam, later_weights, batch, ts):
    t, d = x2.shape
    assert w_in.shape[0] % W_IN_CHUNK_ROWS == 0
    nt = t // batch // ts
    steps = batch * nt
    assert all(w.shape[0] % steps == 0 for w in later_weights), "slabs must tile the weights"
    slabs = [pl.BlockSpec((w.shape[0] // steps, w.shape[1]), lambda b, i: (b * nt + i, 0))
             for w in later_weights]
    group = MXU_DIM // w_r.shape[1]
    conv_w, conv_b = 0.25 * conv_w, 0.25 * conv_b
    wr = (2.0 * _block_diag(w_r, group)).astype(BF16)
    wi = (2.0 * _block_diag(w_i, group)).astype(BF16)
    tile = pl.BlockSpec((ts, d), lambda b, i: (b * nt + i, 0))
    row = _resident((1, d))
    dtypes = [BF16, BF16, BF16, F32, F32, BF16]
    outs = pl.pallas_call(
        _in_lru_kernel,
        grid=(batch, nt),
        in_specs=[tile, row, pl.BlockSpec(memory_space=pl.ANY), _resident(conv_w.shape), row,
                  _resident(wr.shape), row, _resident(wi.shape), row, row] + slabs,
        out_specs=[tile] * len(dtypes) + slabs,
        out_shape=([jax.ShapeDtypeStruct((t, d), dt) for dt in dtypes]
                   + [jax.ShapeDtypeStruct(w.shape, BF16) for w in later_weights]),
        scratch_shapes=[pltpu.VMEM((SUBLANES, d), F32), pltpu.VMEM((ts, d), F32),
                        pltpu.VMEM((ts, d), F32), pltpu.VMEM((ts, d), F32),
                        pltpu.VMEM((1, d), F32), pltpu.VMEM(w_in.shape, BF16),
                        pltpu.VMEM((2, W_IN_CHUNK_ROWS, w_in.shape[1]), F32),
                        pltpu.SemaphoreType.DMA((2,))],
        compiler_params=pltpu.CompilerParams(
            dimension_semantics=("arbitrary", "arbitrary"), vmem_limit_bytes=VMEM_LIMIT),
        name="in_lru",
    )(x2, g, w_in, conv_w, conv_b.reshape(1, d), wr, b_r.reshape(1, d), wi, b_i.reshape(1, d),
      lam.reshape(1, d), *later_weights)
    return outs[:len(dtypes)], outs[len(dtypes):]


SOFTPLUS_CAP = 80.0


def _softplus(z):
    return jnp.maximum(z, jnp.log(1.0 + jnp.exp2(jnp.minimum(z, SOFTPLUS_CAP) * LOG2E)))


LANE_BLOCKS = 8


def _attn_kernel(qp_ref, qn_ref, k_ref, v_ref, tri_ref, o_ref, z_scr, att_scr, carry_scr):
    tq = qp_ref.shape[0]
    tk = tri_ref.shape[1]
    i = pl.program_id(2)
    nq = pl.num_programs(2) - 1
    lane = lax.broadcasted_iota(jnp.int32, (tq, ATTN_LANES), 1)
    tri = tri_ref[...]
    done = jnp.maximum(i - 1, 0)

    def stack_heads(q):
        q = q * (HEAD_DIM ** -0.5)
        zero = jnp.zeros_like(q)
        return jnp.concatenate(
            [jnp.where(lane // HEAD_DIM == h, q, zero) for h in range(HEADS_PER_BLOCK)], axis=0)

    def tile_rows(j):
        return pl.ds(pl.multiple_of(j * tk, tk), tk)

    def scores(q2, k_tiles):
        return lax.dot_general(q2, k_tiles, (((1,), (1,)), ((), ())), preferred_element_type=F32)

    def decay_terms(z):
        pos = _softplus(z)
        pos_bf = pos.astype(BF16)
        later = _bdot(pos_bf, tri)
        return z - pos, later, later[:, 0:1] + pos_bf[:, 0:1].astype(F32)

    def first_scores(q2, j, lanes):
        prev = jnp.maximum(j - 1, 0)
        return scores(q2, jnp.concatenate(
            [k_ref[tile_rows(j), lanes], k_ref[tile_rows(prev), lanes]], axis=0))

    @pl.when(i == 0)
    def _():
        for lb in range(LANE_BLOCKS):
            lanes = slice(lb * ATTN_LANES, (lb + 1) * ATTN_LANES)
            z_scr[lb] = first_scores(stack_heads(qp_ref[:, lanes]), 0, lanes)
        att_scr[...] = jnp.zeros(att_scr.shape, BF16)
        carry_scr[...] = jnp.full(carry_scr.shape, 2.0 * EXP_UNDERFLOW, F32)

    def first_steps(lb):
        lanes = slice(lb * ATTN_LANES, (lb + 1) * ATTN_LANES)

        carry_done = carry_scr[lb]
        v_prev = v_ref[tile_rows(jnp.maximum(done - 1, 0)), lanes]
        v_both = jnp.concatenate(
            [v_ref[tile_rows(done), lanes], jnp.where(done > 0, v_prev, jnp.zeros_like(v_prev))],
            axis=0)
        acc = _bdot(att_scr[lb], v_both)

        z = z_scr[lb]
        z_scr[lb] = first_scores(stack_heads(qn_ref[:, lanes]), jnp.minimum(i + 1, nq - 1),
                                 lanes)
        rq = lax.broadcasted_iota(jnp.int32, (HEADS_PER_BLOCK * tq, tk), 0) % tq
        ck = lax.broadcasted_iota(jnp.int32, (HEADS_PER_BLOCK * tq, tk), 1)
        z_diag = jnp.where(ck < rq, z[:, :tk], MASKED_LOGIT)
        z_prev = z[:, tk:]
        rows = HEADS_PER_BLOCK * tq
        log_beta, later, total = decay_terms(jnp.concatenate([z_diag, z_prev], axis=0))
        carry = total[:rows]
        att_scr[lb] = jnp.concatenate(
            [jnp.exp(log_beta[:rows] - later[:rows]),
             jnp.exp(log_beta[rows:] - (later[rows:] + carry))], axis=1).astype(BF16)
        carry_scr[lb] = carry + total[rows:]
        return acc, carry_done, jnp.min(carry_done)

    def earlier_tiles(lb, acc, carry_done, low):
        lanes = slice(lb * ATTN_LANES, (lb + 1) * ATTN_LANES)

        def cond(state):
            j, low, _, _ = state
            return jnp.logical_and(j >= 0, low < EXP_UNDERFLOW)

        def body(state):
            j, _, acc, carry = state
            z = scores(stack_heads(qp_ref[:, lanes]), k_ref[tile_rows(j), lanes])
            log_beta, later, total = decay_terms(z)
            att = jnp.exp(log_beta - (later + carry))
            acc = acc + _bdot(att.astype(BF16), v_ref[tile_rows(j), lanes])
            carry = carry + total
            return j - 1, jnp.min(carry), acc, carry

        _, _, acc, _ = lax.while_loop(cond, body, (i - 3, low, acc, carry_done))
        out = acc[:tq]
        for h in range(1, HEADS_PER_BLOCK):
            out = jnp.where(lane // HEAD_DIM == h, acc[h * tq:(h + 1) * tq], out)
        o_ref[:, lanes] = out.astype(o_ref.dtype)

    started = [first_steps(lb) for lb in range(LANE_BLOCKS)]
    for lb, (acc, carry_done, low) in enumerate(started):
        earlier_tiles(lb, acc, carry_done, low)


def _attn(q, k, v, batch, tq):
    t, width = q.shape
    seq = t // batch
    nq = seq // tq
    tk = tq
    group = LANE_BLOCKS * ATTN_LANES
    later = jnp.arange(tk)[:, None] > jnp.arange(tk)[None, :]

    def q_tile(shift):
        return pl.BlockSpec(
            (tq, group), lambda b, hp, i: (b * nq + jnp.clip(i + shift, 0, nq - 1), hp))

    kv_spec = pl.BlockSpec((seq, group), lambda b, hp, i: (b, hp), pipeline_mode=pl.Buffered(1))
    rows = HEADS_PER_BLOCK * tq
    return pl.pallas_call(
        _attn_kernel,
        grid=(batch, width // group, nq + 1),
        in_specs=[q_tile(-1), q_tile(1), kv_spec, kv_spec, _resident((tk, tk))],
        out_specs=q_tile(-1),
        out_shape=jax.ShapeDtypeStruct((t, width), BF16),
        scratch_shapes=[pltpu.VMEM((LANE_BLOCKS, rows, 2 * tk), F32),
                        pltpu.VMEM((LANE_BLOCKS, rows, 2 * tk), BF16),
                        pltpu.VMEM((LANE_BLOCKS, rows, 1), F32)],
        compiler_params=pltpu.CompilerParams(
            dimension_semantics=("arbitrary", "arbitrary", "arbitrary"),
            vmem_limit_bytes=VMEM_LIMIT),
        name="attn",
    )(q, q, k, v, later.astype(BF16))


def _post_kernel(x_ref, yl_ref, ya_ref, gl_ref, ga_ref, p_ref, wbl_ref, wba_ref, wo_ref,
                 g_mlp_ref, wu_ref, wd_ref, g_ple_ref, wpg_ref, wpl_ref, g_fin_ref, o_ref):
    merged = (_sigmoid(gl_ref[...]) * _bdot(yl_ref[...], wbl_ref[...])
              + _sigmoid(ga_ref[...]) * _bdot(ya_ref[...], wba_ref[...]))
    x = x_ref[...] + _bdot(merged.astype(BF16), wo_ref[...])
    h2 = _rmsnorm(x, g_mlp_ref[...]).astype(BF16)
    up = jnp.square(jnp.maximum(_bdot(h2, wu_ref[...]), 0.0))
    x = x + _bdot(up.astype(BF16), wd_ref[...])
    h3 = _rmsnorm(x, g_ple_ref[...]).astype(BF16)
    gate = _sigmoid(_bdot(h3, wpg_ref[...]))
    x = x + gate * _bdot(p_ref[...].astype(BF16), wpl_ref[...])
    o_ref[...] = _rmsnorm(x, g_fin_ref[...])


def _post(x2, y_lru, y_att, g_lru, g_att, p2, w_bl, w_ba, w_o, g_mlp, w_up, w_down, g_ple, w_pg,
          w_pl, g_fin, tm):
    t, d = x2.shape
    tile = pl.BlockSpec((tm, d), lambda i: (i, 0))
    ptile = pl.BlockSpec((tm, p2.shape[1]), lambda i: (i, 0))
    row = _resident((1, d))
    square = _resident((d, d))
    return pl.pallas_call(
        _post_kernel,
        grid=(t // tm,),
        in_specs=[tile] * 5 + [ptile, square, square, square, row, _resident(w_up.shape),
                               _resident(w_down.shape), row, square, _resident(w_pl.shape), row],
        out_specs=tile,
        out_shape=jax.ShapeDtypeStruct((t, d), F32),
        compiler_params=pltpu.CompilerParams(
            dimension_semantics=("arbitrary",), vmem_limit_bytes=VMEM_LIMIT),
        name="post",
    )(x2, y_lru, y_att, g_lru, g_att, p2, w_bl, w_ba, w_o, g_mlp, w_up, w_down, g_ple, w_pg,
      w_pl, g_fin)


def kernel(x, p, norm_mix_g, w_in, conv_w, conv_b, w_rgate, b_rgate, w_igate, b_igate, lru_lambda, w_br_lru, w_br_att, w_out, norm_mlp_g, w_mlp_up, w_mlp_down, norm_ple_g, w_ple_gate, w_ple, norm_final_g):
    batch, seq, d = x.shape
    depth = w_in.shape[0]
    t = batch * seq
    x2 = x.reshape(t, d)
    for l in range(depth):
        later = [w_br_lru[l], w_br_att[l], w_out[l], w_mlp_up[l], w_mlp_down[l], w_ple_gate[l]]
        (q, k, v, g_lru, g_att, y_lru), later_bf = _in_lru(
            x2, norm_mix_g[l].reshape(1, d), w_in[l], conv_w[l], conv_b[l],
            w_rgate[l], b_rgate[l], w_igate[l], b_igate[l], lru_lambda[l], later, batch, ts=512)
        w_bl, w_ba, w_o, w_up, w_down, w_pg = later_bf
        y_att = _attn(q, k, v, batch, tq=256)
        assert l == depth - 1, "deeper stacks need a post variant without the final norm"
        x2 = _post(x2, y_lru, y_att, g_lru, g_att, p[l].reshape(t, -1), w_bl, w_ba, w_o,
                   norm_mlp_g[l].reshape(1, d), w_up, w_down, norm_ple_g[l].reshape(1, d),
                   w_pg, w_ple[l].astype(BF16), norm_final_g.reshape(1, d), tm=512)
    return x2.reshape(batch, seq, d)
```
